```python
import jax, jax.numpy as jnp
from jax import lax
import numpy as np

D_MODEL = 1024
BATCH = 16
SEQ = 2048
DEPTH = 2

HEAD_DIM = 64
DIFF_HEADS = 4
DIFF_QK_W = DIFF_HEADS * 2 * HEAD_DIM
DIFF_V_W = DIFF_HEADS * 2 * HEAD_DIM
DIL_HEADS = 8
DIL_W = DIL_HEADS * HEAD_DIM
DIL_PATTERNS = ((128, 1), (512, 4), (2048, 16))
HGRN_HEADS = 4
HGRN_DK = 128
HGRN_DV = 128
HGRN_K_W = HGRN_HEADS * HGRN_DK
HGRN_V_W = HGRN_HEADS * HGRN_DV
HGRN_CHUNK = 64
N_BRANCH = 3
D_FF = 2816
Q_BLOCK = 128
NORM_EPS = 1e-6
N_ALIBI = DIFF_HEADS + DIL_HEADS
ALIBI_A_IDX = (0, 3, 6, 9)
ALIBI_B_IDX = (1, 2, 4, 5, 7, 8, 10, 11)
IN_SIZES = (DIFF_QK_W, DIFF_QK_W, DIFF_V_W, DIL_W, DIL_W, DIL_W,
            HGRN_K_W, HGRN_K_W, HGRN_K_W, HGRN_V_W, HGRN_V_W, N_BRANCH * D_MODEL)
D_IN = sum(IN_SIZES)

kernel_name = "hybrid_diffattn_dilated_hgrn2_gated_encoder"


def rms_norm(x, g):
    xf = x.astype(jnp.float32)
    y = xf * lax.rsqrt(jnp.mean(xf * xf, axis=-1, keepdims=True) + NORM_EPS)
    return (y * g.astype(jnp.float32)).astype(x.dtype)


def swiglu(x, w_gate, w_up, w_down):
    return (jax.nn.silu(x @ w_gate) * (x @ w_up)) @ w_down


def alibi_slopes():
    h = jnp.arange(1, N_ALIBI + 1, dtype=jnp.float32)
    s = jnp.exp2(-8.0 * h / N_ALIBI)
    return s[jnp.array(ALIBI_A_IDX)], s[jnp.array(ALIBI_B_IDX)]


def split_columns(z):
    parts = []
    start = 0
    for size in IN_SIZES:
        parts.append(z[..., start:start + size])
        start += size
    return parts


def diff_attention(q, k, v, slopes, lam, subln, lam_init):
    B, S, H = q.shape[0], q.shape[1], q.shape[2]
    scale = HEAD_DIM ** -0.5
    qh = jnp.transpose(q, (0, 2, 3, 1, 4))
    kh = jnp.transpose(k, (0, 2, 3, 1, 4))
    vh = jnp.transpose(v, (0, 2, 1, 3))
    nblk = S // Q_BLOCK
    qb = jnp.moveaxis(qh.reshape(B, H, 2, nblk, Q_BLOCK, HEAD_DIM), 3, 0)
    kpos = jnp.arange(S)

    def block(args):
        qblk, bi = args
        qpos = bi * Q_BLOCK + jnp.arange(Q_BLOCK)
        dist = jnp.abs(qpos[:, None] - kpos[None, :]).astype(jnp.float32)
        bias = -slopes[:, None, None] * dist
        s = jnp.einsum('bhmqd,bhmkd->bhmqk', qblk, kh).astype(jnp.float32) * scale
        p = jax.nn.softmax(s + bias[None, :, None], axis=-1)
        pd = p[:, :, 0] - lam * p[:, :, 1]
        return jnp.einsum('bhqk,bhkv->bhqv', pd.astype(vh.dtype), vh)

    o = lax.map(block, (qb, jnp.arange(nblk)))
    o = jnp.moveaxis(o, 0, 2).reshape(B, H, S, 2 * HEAD_DIM)
    o = (rms_norm(o, subln) * (1.0 - lam_init)).astype(q.dtype)
    return jnp.transpose(o, (0, 2, 1, 3)).reshape(B, S, H * 2 * HEAD_DIM)


def dilated_pattern(q, k, v, slopes, window, dilation):
    B, H, S, dh = q.shape
    R = window // (2 * dilation)
    L = S // dilation
    nb = -(-L // R)
    Lp = nb * R
    scale = dh ** -0.5

    def to_sub(t):
        return jnp.swapaxes(t.reshape(B, H, L, dilation, dh), 2, 3)

    qs = jnp.pad(to_sub(q), ((0, 0), (0, 0), (0, 0), (0, Lp - L), (0, 0)))
    qs = qs.reshape(B, H, dilation, nb, R, dh)

    def band(t):
        tp = jnp.pad(to_sub(t), ((0, 0), (0, 0), (0, 0), (R, Lp - L + R), (0, 0)))
        tb = tp.reshape(B, H, dilation, nb + 2, R, dh)
        return jnp.concatenate([tb[:, :, :, :-2], tb[:, :, :, 1:-1], tb[:, :, :, 2:]], axis=4)

    kb, vb = band(k), band(v)
    qi = jnp.arange(nb)[:, None] * R + jnp.arange(R)[None, :]
    ki = jnp.arange(nb)[:, None] * R + jnp.arange(3 * R)[None, :] - R
    rel = jnp.abs(qi[:, :, None] - ki[:, None, :])
    valid = (rel <= R) & (ki[:, None, :] >= 0) & (ki[:, None, :] < L)
    bias = -slopes[:, None, None, None] * (rel * dilation).astype(jnp.float32)
    s = jnp.einsum('bhrnqd,bhrnkd->bhrnqk', qs, kb).astype(jnp.float32) * scale + bias[None, :, None]
    s = jnp.where(valid, s, -jnp.inf)
    lse = jax.nn.logsumexp(s, axis=-1)
    p = jnp.exp(s - lse[..., None])
    o = jnp.einsum('bhrnqk,bhrnkd->bhrnqd', p.astype(v.dtype), vb)
    o = o.reshape(B, H, dilation, Lp, dh)[:, :, :, :L]
    lse = lse.reshape(B, H, dilation, Lp)[:, :, :, :L]
    o = jnp.swapaxes(o, 2, 3).reshape(B, H, S, dh)
    lse = jnp.swapaxes(lse, 2, 3).reshape(B, H, S)
    return o, lse


def dilated_attention(q, k, v, slopes):
    B, S, _ = q.shape
    heads = lambda t: jnp.transpose(t.reshape(B, S, DIL_HEADS, HEAD_DIM), (0, 2, 1, 3))
    qh, kh, vh = heads(q), heads(k), heads(v)
    outs, lses = [], []
    for window, dilation in DIL_PATTERNS:
        o, lse = dilated_pattern(qh, kh, vh, slopes, window, dilation)
        outs.append(o)
        lses.append(lse)
    wts = jax.nn.softmax(jnp.stack(lses, axis=0), axis=0)
    o = jnp.einsum('pbhs,pbhsd->bhsd', wts, jnp.stack(outs, axis=0).astype(jnp.float32))
    return jnp.transpose(o, (0, 2, 1, 3)).reshape(B, S, DIL_W).astype(q.dtype)


def gla_chunk_scan(q, k, v, g):
    B, H, S, dk = q.shape
    dv = v.shape[-1]
    C = HGRN_CHUNK
    n = S // C
    chunks = lambda t: jnp.moveaxis(t.reshape(B, H, n, C, t.shape[-1]), 2, 0)
    causal = jnp.tril(jnp.ones((C, C), dtype=bool))

    def step(state, inp):
        qc, kc, vc, gc = inp
        bc = jnp.cumsum(gc, axis=2)
        o_inter = jnp.einsum('bhck,bhkv->bhcv', qc * jnp.exp(bc), state)
        diff = bc[:, :, :, None, :] - bc[:, :, None, :, :]
        decay = jnp.exp(jnp.where(causal[:, :, None], diff, -jnp.inf))
        a = jnp.einsum('bhtk,bhsk,bhtsk->bhts', qc, kc, decay)
        o_intra = jnp.einsum('bhts,bhsv->bhtv', a, vc)
        btot = bc[:, :, -1]
        state = jnp.exp(btot)[..., None] * state + jnp.einsum(
            'bhck,bhcv->bhkv', kc * jnp.exp(btot[:, :, None] - bc), vc)
        return state, o_inter + o_intra

    s0 = jnp.zeros((B, H, dk, dv), jnp.float32)
    _, o = lax.scan(step, s0, (chunks(q), chunks(k), chunks(v), chunks(g)))
    return jnp.moveaxis(o, 0, 2).reshape(B, H, S, dv)


def hgrn_direction(q, f_logit, i, lb):
    g = jnp.log(lb + (1.0 - lb) * jax.nn.sigmoid(f_logit))
    k = (1.0 - lb) * jax.nn.sigmoid(-f_logit)
    return gla_chunk_scan(q, k, i, g)


def hgrn2_mixer(q, f_fw, f_bw, i, og, lb, norm_g):
    B, S, _ = q.shape
    heads = lambda t, d: jnp.transpose(t.reshape(B, S, HGRN_HEADS, d), (0, 2, 1, 3)).astype(jnp.float32)
    qh, ih = heads(q, HGRN_DK), heads(i, HGRN_DV)
    ffh, fbh = heads(f_fw, HGRN_DK), heads(f_bw, HGRN_DK)
    lbh = lb.astype(jnp.float32).reshape(HGRN_HEADS, 1, HGRN_DK)
    o_fw = hgrn_direction(qh, ffh, ih, lbh)
    rev = lambda t: jnp.flip(t, axis=2)
    o_bw = rev(hgrn_direction(rev(qh), rev(fbh), rev(ih), lbh))
    o = rms_norm(o_fw + o_bw, norm_g) * jax.nn.silu(heads(og, HGRN_DV))
    return jnp.transpose(o, (0, 2, 1, 3)).reshape(B, S, HGRN_V_W).astype(q.dtype)


def setup_inputs(seed: int = 0) -> dict:
    key = jax.random.key(seed)
    ks = jax.random.split(key, 24)
    f32 = jnp.float32
    dense = lambda k, shape, fan_in: jax.random.normal(k, shape, f32) * fan_in ** -0.5
    gain = lambda k, shape: 1.0 + 0.02 * jax.random.normal(k, shape, f32)
    return {
        "x": jax.random.normal(ks[0], (BATCH, SEQ, D_MODEL), f32),
        "ffn1_norm": gain(ks[1], (DEPTH, D_MODEL)),
        "ffn1_w_gate": dense(ks[2], (DEPTH, D_MODEL, D_FF), D_MODEL),
        "ffn1_w_up": dense(ks[3], (DEPTH, D_MODEL, D_FF), D_MODEL),
        "ffn1_w_down": dense(ks[4], (DEPTH, D_FF, D_MODEL), D_FF),
        "mix_norm": gain(ks[5], (DEPTH, D_MODEL)),
        "w_in": dense(ks[6], (DEPTH, D_MODEL, D_IN), D_MODEL),
        "diff_lambda_q1": 0.1 * jax.random.normal(ks[7], (DEPTH, HEAD_DIM), f32),
        "diff_lambda_k1": 0.1 * jax.random.normal(ks[8], (DEPTH, HEAD_DIM), f32),
        "diff_lambda_q2": 0.1 * jax.random.normal(ks[9], (DEPTH, HEAD_DIM), f32),
        "diff_lambda_k2": 0.1 * jax.random.normal(ks[10], (DEPTH, HEAD_DIM), f32),
        "diff_subln": gain(ks[11], (DEPTH, 2 * HEAD_DIM)),
        "hgrn_lower_bounds": 0.1 * jax.random.normal(ks[12], (DEPTH, HGRN_K_W), f32),
        "hgrn_norm": gain(ks[13], (DEPTH, HGRN_DV)),
        "w_branch_a": dense(ks[14], (DEPTH, DIFF_V_W, D_MODEL), DIFF_V_W),
        "w_branch_b": dense(ks[15], (DEPTH, DIL_W, D_MODEL), DIL_W),
        "w_branch_c": dense(ks[16], (DEPTH, HGRN_V_W, D_MODEL), HGRN_V_W),
        "w_out": dense(ks[17], (DEPTH, D_MODEL, D_MODEL), D_MODEL),
        "ffn2_norm": gain(ks[18], (DEPTH, D_MODEL)),
        "ffn2_w_gate": dense(ks[19], (DEPTH, D_MODEL, D_FF), D_MODEL),
        "ffn2_w_up": dense(ks[20], (DEPTH, D_MODEL, D_FF), D_MODEL),
        "ffn2_w_down": dense(ks[21], (DEPTH, D_FF, D_MODEL), D_FF),
        "final_norm": gain(ks[22], (D_MODEL,)),
    }


def reference(x, ffn1_norm, ffn1_w_gate, ffn1_w_up, ffn1_w_down, mix_norm, w_in,
              diff_lambda_q1, diff_lambda_k1, diff_lambda_q2, diff_lambda_k2, diff_subln,
              hgrn_lower_bounds, hgrn_norm, w_branch_a, w_branch_b, w_branch_c, w_out,
              ffn2_norm, ffn2_w_gate, ffn2_w_up, ffn2_w_down, final_norm):
    B, S, _ = x.shape
    slopes_a, slopes_b = alibi_slopes()
    lb_p = jax.nn.softmax(hgrn_lower_bounds.astype(jnp.float32), axis=0)
    lb_all = jnp.cumsum(lb_p, axis=0) - lb_p[0]
    h = x
    for l in range(DEPTH):
        h = h + 0.5 * swiglu(rms_norm(h, ffn1_norm[l]), ffn1_w_gate[l], ffn1_w_up[l], ffn1_w_down[l])
        u = rms_norm(h, mix_norm[l])
        z = u @ w_in[l]
        a_q, a_k, a_v, b_q, b_k, b_v, c_q, c_ffw, c_fbw, c_i, c_og, gates = split_columns(z)
        lam_init = 0.8 - 0.6 * float(np.exp(-0.3 * l))
        lam = (jnp.exp(jnp.sum(diff_lambda_q1[l] * diff_lambda_k1[l]).astype(jnp.float32))
               - jnp.exp(jnp.sum(diff_lambda_q2[l] * diff_lambda_k2[l]).astype(jnp.float32)) + lam_init)
        ya = diff_attention(a_q.reshape(B, S, DIFF_HEADS, 2, HEAD_DIM),
                            a_k.reshape(B, S, DIFF_HEADS, 2, HEAD_DIM),
                            a_v.reshape(B, S, DIFF_HEADS, 2 * HEAD_DIM),
                            slopes_a, lam, diff_subln[l], lam_init)
        yb = dilated_attention(b_q, b_k, b_v, slopes_b)
        yc = hgrn2_mixer(c_q, c_ffw, c_fbw, c_i, c_og, lb_all[l], hgrn_norm[l])
        g = jax.nn.sigmoid(gates.astype(jnp.float32)).reshape(B, S, N_BRANCH, D_MODEL).astype(h.dtype)
        merged = (g[:, :, 0] * (ya @ w_branch_a[l]) + g[:, :, 1] * (yb @ w_branch_b[l])
                  + g[:, :, 2] * (yc @ w_branch_c[l]))
        h = h + merged @ w_out[l]
        h = h + 0.5 * swiglu(rms_norm(h, ffn2_norm[l]), ffn2_w_gate[l], ffn2_w_up[l], ffn2_w_down[l])
    return rms_norm(h, final_norm)
```

```python
import functools

import numpy as np
import jax
import jax.numpy as jnp
from jax import lax
from jax.experimental import pallas as pl
from jax.experimental.pallas import tpu as pltpu

F32 = jnp.float32
BF16 = jnp.bfloat16

D_MODEL = 1024
HEAD_DIM = 64
DIFF_HEADS = 4
DIL_HEADS = 8
DIL_PATTERNS = ((128, 1), (512, 4), (2048, 16))
HGRN_HEADS = 4
HGRN_DK = 128
N_BRANCH = 3
D_FF = 2816
NORM_EPS = 1e-6
N_ALIBI = DIFF_HEADS + DIL_HEADS
ALIBI_A_IDX = (0, 3, 6, 9)
ALIBI_B_IDX = (1, 2, 4, 5, 7, 8, 10, 11)
ATT_W = 6 * 512
HG_W = 5 * 512
GATE_W = N_BRANCH * D_MODEL

LANES = 128
VMEM_LIMIT = 56 * 1024 * 1024

FFN_TM = 512
MM_TM = 1024
ATT_TQ = 256
HGRN_C = 128
NEG_BIG = -1e30


def _cparams(sem):
    return pltpu.CompilerParams(dimension_semantics=sem, vmem_limit_bytes=VMEM_LIMIT)


def _resident(shape):
    nd = len(shape)
    return pl.BlockSpec(shape, lambda *_: (0,) * nd, pipeline_mode=pl.Buffered(1))


def _rms(x, g):
    return x * lax.rsqrt(jnp.mean(x * x, axis=-1, keepdims=True) + NORM_EPS) * g


def _ffn_kernel(h_ref, g1_ref, wg_ref, wu_ref, wd_ref, g2_ref, *out_refs, emit):
    h = h_ref[...]
    n = _rms(h, g1_ref[...]).astype(BF16)
    a = jnp.dot(n, wg_ref[...], preferred_element_type=F32)
    b = jnp.dot(n, wu_ref[...], preferred_element_type=F32)
    t = (a * jax.nn.sigmoid(a) * b).astype(BF16)
    hout = h + 0.5 * jnp.dot(t, wd_ref[...], preferred_element_type=F32)
    if "h" in emit:
        out_refs[0][...] = hout
    if "n" in emit:
        out_refs[-1][...] = _rms(hout, g2_ref[...]).astype(out_refs[-1].dtype)


def ffn_block(h, g1, wg, wu, wd, g2, emit, n_dtype=F32):
    T, D = h.shape
    tm = min(FFN_TM, T)
    row = pl.BlockSpec((tm, D), lambda i: (i, 0))
    out_dtypes = [F32] * ("h" in emit) + [n_dtype] * ("n" in emit)
    return pl.pallas_call(
        functools.partial(_ffn_kernel, emit=emit),
        grid=(T // tm,),
        in_specs=[row, _resident((1, D)), _resident(wg.shape), _resident(wu.shape),
                  _resident(wd.shape), _resident((1, D))],
        out_specs=[row] * len(out_dtypes),
        out_shape=[jax.ShapeDtypeStruct((T, D), dt) for dt in out_dtypes],
        compiler_params=_cparams(("parallel",)),
        name="ffn",
    )(h, g1.reshape(1, D), wg, wu, wd, g2.reshape(1, D))


def _proj_kernel(x_ref, w_ref, o_ref, *, act):
    acc = jnp.dot(x_ref[...], w_ref[...], preferred_element_type=F32)
    if act == "sigmoid":
        acc = jax.nn.sigmoid(acc)
    o_ref[...] = acc.astype(o_ref.dtype)


def project(x, w, out_dtype, tn, act=None):
    T, K = x.shape
    N = w.shape[1]
    tm = min(MM_TM, T)
    return pl.pallas_call(
        functools.partial(_proj_kernel, act=act),
        grid=(T // tm, N // tn),
        in_specs=[pl.BlockSpec((tm, K), lambda i, j: (i, 0)),
                  pl.BlockSpec((K, tn), lambda i, j: (0, j))],
        out_specs=pl.BlockSpec((tm, tn), lambda i, j: (i, j)),
        out_shape=jax.ShapeDtypeStruct((T, N), out_dtype),
        compiler_params=_cparams(("parallel", "arbitrary")),
        name="proj_in",
    )(x, w)


def _attn_kernel(q_ref, k_ref, v_ref, strip_ref, lamv_ref, subln_ref, o_ref, *, mode, lam_init):
    tq = q_ref.shape[1]
    S = k_ref.shape[1]
    qi = pl.program_id(2)
    q = q_ref[0]
    k = k_ref[0]
    v = v_ref[0]
    lane = lax.broadcasted_iota(jnp.int32, (tq, LANES), 1)
    first = lane < HEAD_DIM
    scale = jnp.asarray(HEAD_DIM ** -0.5, BF16)
    zero = jnp.zeros_like(q)
    halves = (jnp.where(first, q, zero) * scale, jnp.where(first, zero, q) * scale)
    off = pl.multiple_of(S - qi * tq, LANES)
    outs = []
    for e, qq in enumerate(halves):
        s = lax.dot_general(qq, k, (((1,), (1,)), ((), ())), preferred_element_type=F32)
        s = s + strip_ref[0, e, :, pl.ds(off, S)]
        m = jnp.max(s, axis=-1, keepdims=True)
        p = jnp.exp(s - m)
        l = jnp.sum(p, axis=-1, keepdims=True)
        o = jnp.dot(p.astype(BF16), v, preferred_element_type=F32)
        outs.append(o / l)
    if mode == "diff":
        lv = lamv_ref[...]
        lam = (jnp.exp(jnp.sum(lv[0:1] * lv[1:2], axis=-1, keepdims=True))
               - jnp.exp(jnp.sum(lv[2:3] * lv[3:4], axis=-1, keepdims=True)) + lam_init)
        o = outs[0] - lam * outs[1]
        o = _rms(o, subln_ref[...]) * (1.0 - lam_init)
    else:
        o = jnp.where(first, outs[0], outs[1])
    o_ref[0] = o.astype(o_ref.dtype)


def attention_pairs(zatt, strips, lamv, subln, *, mode, lam_init, q_blk, k_blk, v_blk):
    B, S, _ = zatt.shape
    n_pairs = strips.shape[0]
    tq = strips.shape[2]
    return pl.pallas_call(
        functools.partial(_attn_kernel, mode=mode, lam_init=lam_init),
        grid=(n_pairs, B, S // tq),
        in_specs=[
            pl.BlockSpec((1, tq, LANES), lambda p, b, i: (b, i, q_blk + p)),
            pl.BlockSpec((1, S, LANES), lambda p, b, i: (b, 0, k_blk + p)),
            pl.BlockSpec((1, S, LANES), lambda p, b, i: (b, 0, v_blk + p)),
            pl.BlockSpec((1, 2, tq, 2 * S), lambda p, b, i: (p, 0, 0, 0),
                         pipeline_mode=pl.Buffered(1)),
            _resident(lamv.shape),
            _resident(subln.shape),
        ],
        out_specs=pl.BlockSpec((1, tq, LANES), lambda p, b, i: (b, i, p)),
        out_shape=jax.ShapeDtypeStruct((B, S, n_pairs * LANES), BF16),
        compiler_params=_cparams(("parallel", "parallel", "arbitrary")),
        name="attn_" + mode,
    )(zatt, zatt, zatt, strips, lamv, subln)


def _alibi_slopes():
    h = np.arange(1, N_ALIBI + 1, dtype=np.float64)
    s = np.exp2(-8.0 * h / N_ALIBI)
    return s[list(ALIBI_A_IDX)], s[list(ALIBI_B_IDX)]


def _bias_strips(S, tq):
    sl_a, sl_b = _alibi_slopes()
    r = np.arange(tq, dtype=np.float64)[:, None]
    c = np.arange(2 * S, dtype=np.float64)[None, :]
    delta = np.abs(r - (c - S))
    strips_a = np.stack([np.stack([-s * delta, -s * delta]) for s in sl_a])
    mult = np.zeros_like(delta)
    for window, dilation in DIL_PATTERNS:
        reach = (window // (2 * dilation)) * dilation
        mult += ((delta % dilation) == 0) & (delta <= reach)
    logm = np.where(mult > 0, np.log(np.maximum(mult, 1.0)), NEG_BIG)
    strips_b = np.stack([np.stack([-sl_b[2 * p] * delta + logm, -sl_b[2 * p + 1] * delta + logm])
                         for p in range(DIL_HEADS // 2)])
    return jnp.asarray(strips_a, F32), jnp.asarray(strips_b, F32)


def _hgrn_tables(C):
    levels = int(np.log2(C))
    t = np.arange(C)
    sums = np.zeros((2, levels + 1, C, C), np.float32)
    level = -np.ones((2, C, C), np.int32)
    for l in range(levels):
        m = 1 << l
        start = (t // (2 * m)) * (2 * m)
        upper = (t % (2 * m)) >= m
        same = (t[:, None] // (2 * m)) == (t[None, :] // (2 * m))
        mid = start + m - 1
        r = t[None, :]
        lo = (~upper)[:, None] & (r > t[:, None]) & (r <= mid[:, None])
        up = upper[:, None] & (r > mid[:, None]) & (r <= t[:, None])
        sums[0, l] = lo | up
        level[0][same & upper[:, None] & (~upper)[None, :]] = l
        mid2 = start + m
        lo = (~upper)[:, None] & (r >= t[:, None]) & (r < mid2[:, None])
        up = upper[:, None] & (r >= mid2[:, None]) & (r < t[:, None])
        sums[1, l] = lo | up
        level[1][same & (~upper)[:, None] & upper[None, :]] = l
    sums[0, levels] = t[None, :] <= t[:, None]
    sums[1, levels] = t[None, :] >= t[:, None]
    return sums.reshape(2, (levels + 1) * C, C), level


def _hgrn_kernel(q_ref, ff_ref, fb_ref, i_ref, og_ref, lbp_ref, ng_ref, sums_ref, level_ref,
                 o_ref, acc_ref, st_ref, *, layer):
    S = q_ref.shape[1]
    C = HGRN_C
    n_chunks = S // C
    levels = sums_ref.shape[1] // C - 1

    lbp = lbp_ref[...]
    e = jnp.exp(lbp - jnp.max(lbp, axis=0, keepdims=True))
    pr = e / jnp.sum(e, axis=0, keepdims=True)
    lb = jnp.zeros((1, LANES), F32)
    for j in range(1, layer + 1):
        lb = lb + pr[j:j + 1]

    st_ref[...] = jnp.zeros_like(st_ref)

    def one_direction(d, rows, q, v, vt_b, v_b, f_ref):
        x = f_ref[0, rows, :]
        sg = jax.nn.sigmoid(x)
        g = jnp.log(lb + (1.0 - lb) * sg)
        kk = (1.0 - lb) * jax.nn.sigmoid(-x)
        g0 = g.astype(BF16)
        r1 = g - g0.astype(F32)
        g1 = r1.astype(BF16)
        g2 = (r1 - g1.astype(F32)).astype(BF16)
        sm = sums_ref[d]
        dec = (jnp.dot(sm, g0, preferred_element_type=F32)
               + jnp.dot(sm, g1, preferred_element_type=F32)
               + jnp.dot(sm, g2, preferred_element_type=F32))
        lev = level_ref[d]
        a = jnp.zeros((C, C), F32)
        for l in range(levels):
            ex = jnp.exp(dec[l * C:(l + 1) * C])
            r = lax.dot_general((q * ex).astype(BF16), (kk * ex).astype(BF16),
                                (((1,), (1,)), ((), ())), preferred_element_type=F32)
            a = jnp.where(lev == l, r, a)
        run = dec[levels * C:]
        tot = run[C - 1:C] if d == 0 else run[0:1]
        st = st_ref[d]
        qe = (q * jnp.exp(run)).astype(BF16)
        o = lax.dot_general(qe, st.astype(BF16), (((1,), (1,)), ((), ())),
                            preferred_element_type=F32)
        o = o + jnp.dot(a.astype(BF16), v_b, preferred_element_type=F32)
        o = o + jnp.sum(q * kk, axis=-1, keepdims=True) * v
        kh = (kk * jnp.exp(tot - run)).astype(BF16)
        st_ref[d] = st * jnp.exp(tot) + jnp.dot(vt_b, kh, preferred_element_type=F32)
        return o

    def body(c, carry):
        for d in range(2):
            cc = c if d == 0 else n_chunks - 1 - c
            rows = pl.ds(pl.multiple_of(cc * C, C), C)
            q = q_ref[0, rows, :]
            v = i_ref[0, rows, :]
            v_b = v.astype(BF16)
            vt_b = v.T.astype(BF16)
            o = one_direction(d, rows, q, v, vt_b, v_b, ff_ref if d == 0 else fb_ref)
            acc_ref[d, rows, :] = o
        return carry

    lax.fori_loop(0, n_chunks, body, 0)
    og = og_ref[0]
    o = _rms(acc_ref[0] + acc_ref[1], ng_ref[...]) * (og * jax.nn.sigmoid(og))
    o_ref[0] = o.astype(o_ref.dtype)


def hgrn_mixer(zhg, lower_bounds, norm_g, layer):
    B, S, _ = zhg.shape
    sums, level = _hgrn_tables(HGRN_C)
    sums = jnp.asarray(sums, BF16)
    level = jnp.asarray(level)
    depth = lower_bounds.shape[0]
    col = lambda g: pl.BlockSpec((1, S, LANES), lambda b, h: (b, 0, g * HGRN_HEADS + h))
    return pl.pallas_call(
        functools.partial(_hgrn_kernel, layer=layer),
        grid=(B, HGRN_HEADS),
        in_specs=[col(0), col(1), col(2), col(3), col(4),
                  pl.BlockSpec((depth, LANES), lambda b, h: (0, h)),
                  _resident((1, LANES)), _resident(sums.shape), _resident(level.shape)],
        out_specs=pl.BlockSpec((1, S, LANES), lambda b, h: (b, 0, h)),
        out_shape=jax.ShapeDtypeStruct((B, S, HGRN_HEADS * LANES), BF16),
        scratch_shapes=[pltpu.VMEM((2, S, LANES), F32), pltpu.VMEM((2, LANES, LANES), F32)],
        compiler_params=_cparams(("parallel", "parallel")),
        name="hgrn",
    )(zhg, zhg, zhg, zhg, zhg, lower_bounds, norm_g.reshape(1, LANES), sums, level)


def _merge_kernel(h_ref, ya_ref, yb_ref, yc_ref, g_ref, wa_ref, wb_ref, wc_ref, wo_ref, o_ref):
    D = h_ref.shape[1]
    merged = (g_ref[:, 0:D] * jnp.dot(ya_ref[...], wa_ref[...], preferred_element_type=F32)
              + g_ref[:, D:2 * D] * jnp.dot(yb_ref[...], wb_ref[...], preferred_element_type=F32)
              + g_ref[:, 2 * D:3 * D] * jnp.dot(yc_ref[...], wc_ref[...], preferred_element_type=F32))
    o_ref[...] = h_ref[...] + jnp.dot(merged.astype(BF16), wo_ref[...], preferred_element_type=F32)


def merge_block(h, ya, yb, yc, gates, wa, wb, wc, wo):
    T, D = h.shape
    tm = min(FFN_TM, T)
    row = lambda w: pl.BlockSpec((tm, w), lambda i: (i, 0))
    return pl.pallas_call(
        _merge_kernel,
        grid=(T // tm,),
        in_specs=[row(D), row(ya.shape[1]), row(yb.shape[1]), row(yc.shape[1]), row(gates.shape[1]),
                  _resident(wa.shape), _resident(wb.shape), _resident(wc.shape), _resident(wo.shape)],
        out_specs=row(D),
        out_shape=jax.ShapeDtypeStruct((T, D), F32),
        compiler_params=_cparams(("parallel",)),
        name="merge",
    )(h, ya, yb, yc, gates, wa, wb, wc, wo)


def kernel(x, ffn1_norm, ffn1_w_gate, ffn1_w_up, ffn1_w_down, mix_norm, w_in, diff_lambda_q1, diff_lambda_k1, diff_lambda_q2, diff_lambda_k2, diff_subln, hgrn_lower_bounds, hgrn_norm, w_branch_a, w_branch_b, w_branch_c, w_out, ffn2_norm, ffn2_w_gate, ffn2_w_up, ffn2_w_down, final_norm):
    B, S, D = x.shape
    T = B * S
    depth = w_in.shape[0]
    bf = lambda w: w.astype(BF16)
    strips_a, strips_b = _bias_strips(S, min(ATT_TQ, S))
    h = x.reshape(T, D)
    for l in range(depth):
        h, u = ffn_block(h, ffn1_norm[l], bf(ffn1_w_gate[l]), bf(ffn1_w_up[l]), bf(ffn1_w_down[l]),
                         mix_norm[l], "hn", BF16)
        w = bf(w_in[l])
        zatt = project(u, w[:, :ATT_W], BF16, tn=ATT_W // 2).reshape(B, S, ATT_W)
        zhg = project(u, w[:, ATT_W:ATT_W + HG_W], F32, tn=HG_W // 2).reshape(B, S, HG_W)
        gates = project(u, w[:, ATT_W + HG_W:], F32, tn=GATE_W // 2, act="sigmoid")
        lam_init = 0.8 - 0.6 * float(np.exp(-0.3 * l))
        lamv = jnp.stack([diff_lambda_q1[l], diff_lambda_k1[l], diff_lambda_q2[l], diff_lambda_k2[l]])
        subln = diff_subln[l].reshape(1, 2 * HEAD_DIM)
        ya = attention_pairs(zatt, strips_a, lamv, subln, mode="diff", lam_init=lam_init,
                             q_blk=0, k_blk=4, v_blk=8)
        yb = attention_pairs(zatt, strips_b, lamv, subln, mode="dil", lam_init=lam_init,
                             q_blk=12, k_blk=16, v_blk=20)
        yc = hgrn_mixer(zhg, hgrn_lower_bounds, hgrn_norm[l], l)
        h = merge_block(h, ya.reshape(T, -1), yb.reshape(T, -1), yc.reshape(T, -1), gates,
                        bf(w_branch_a[l]), bf(w_branch_b[l]), bf(w_branch_c[l]), bf(w_out[l]))
        (h,) = ffn_block(h, ffn2_norm[l], bf(ffn2_w_gate[l]), bf(ffn2_w_up[l]), bf(ffn2_w_down[l]),
                         final_norm, "n" if l == depth - 1 else "h")
    return h.reshape(B, S, D)
```

```python
import functools

import numpy as np
import jax
import jax.numpy as jnp
from jax import lax
from jax.experimental import pallas as pl
from jax.experimental.pallas import tpu as pltpu

F32 = jnp.float32
BF16 = jnp.bfloat16

D_MODEL = 1024
HEAD_DIM = 64
DIFF_HEADS = 4
DIL_HEADS = 8
DIL_PATTERNS = ((128, 1), (512, 4), (2048, 16))
HGRN_HEADS = 4
HGRN_DK = 128
N_BRANCH = 3
D_FF = 2816
NORM_EPS = 1e-6
N_ALIBI = DIFF_HEADS + DIL_HEADS
ALIBI_A_IDX = (0, 3, 6, 9)
ALIBI_B_IDX = (1, 2, 4, 5, 7, 8, 10, 11)
ATT_W = 6 * 512
HG_W = 5 * 512
GATE_W = N_BRANCH * D_MODEL

LANES = 128
VMEM_LIMIT = 56 * 1024 * 1024

FFN_TM = 512
MM_TM = 1024
ATT_TQ = 512
ATT_KC = 512
HGRN_C = 128
NEG_BIG = -1e30
LOG2E = float(np.log2(np.e))


def _cparams(sem):
    return pltpu.CompilerParams(dimension_semantics=sem, vmem_limit_bytes=VMEM_LIMIT)


def _resident(shape):
    nd = len(shape)
    return pl.BlockSpec(shape, lambda *_: (0,) * nd, pipeline_mode=pl.Buffered(1))


def _rms(x, g):
    return x * lax.rsqrt(jnp.mean(x * x, axis=-1, keepdims=True) + NORM_EPS) * g


def _ffn_kernel(h_ref, g1_ref, wg_ref, wu_ref, wd_ref, g2_ref, *out_refs, emit):
    h = h_ref[...]
    n = _rms(h, g1_ref[...]).astype(BF16)
    a = jnp.dot(n, wg_ref[...], preferred_element_type=F32)
    b = jnp.dot(n, wu_ref[...], preferred_element_type=F32)
    t = (a * jax.nn.sigmoid(a) * b).astype(BF16)
    hout = h + 0.5 * jnp.dot(t, wd_ref[...], preferred_element_type=F32)
    if "h" in emit:
        out_refs[0][...] = hout
    if "n" in emit:
        out_refs[-1][...] = _rms(hout, g2_ref[...]).astype(out_refs[-1].dtype)


def ffn_block(h, g1, wg, wu, wd, g2, emit, n_dtype=F32):
    T, D = h.shape
    tm = min(FFN_TM, T)
    row = pl.BlockSpec((tm, D), lambda i: (i, 0))
    out_dtypes = [F32] * ("h" in emit) + [n_dtype] * ("n" in emit)
    return pl.pallas_call(
        functools.partial(_ffn_kernel, emit=emit),
        grid=(T // tm,),
        in_specs=[row, _resident((1, D)), _resident(wg.shape), _resident(wu.shape),
                  _resident(wd.shape), _resident((1, D))],
        out_specs=[row] * len(out_dtypes),
        out_shape=[jax.ShapeDtypeStruct((T, D), dt) for dt in out_dtypes],
        compiler_params=_cparams(("parallel",)),
        name="ffn",
    )(h, g1.reshape(1, D), wg, wu, wd, g2.reshape(1, D))


def _proj_kernel(x_ref, w_ref, o_ref, *, act):
    acc = jnp.dot(x_ref[...], w_ref[...], preferred_element_type=F32)
    if act == "sigmoid":
        acc = jax.nn.sigmoid(acc)
    o_ref[...] = acc.astype(o_ref.dtype)


def project(x, w, out_dtype, tn, act=None):
    T, K = x.shape
    N = w.shape[1]
    tm = min(MM_TM, T)
    return pl.pallas_call(
        functools.partial(_proj_kernel, act=act),
        grid=(T // tm, N // tn),
        in_specs=[pl.BlockSpec((tm, K), lambda i, j: (i, 0)),
                  pl.BlockSpec((K, tn), lambda i, j: (0, j))],
        out_specs=pl.BlockSpec((tm, tn), lambda i, j: (i, j)),
        out_shape=jax.ShapeDtypeStruct((T, N), out_dtype),
        compiler_params=_cparams(("parallel", "arbitrary")),
        name="proj_in",
    )(x, w)


def _attn_kernel(q_ref, k_ref, v_ref, strip_ref, lamv_ref, subln_ref, o_ref, vx_ref, s_ref, *,
                 mode, lam_init):
    tq = q_ref.shape[1]
    S = k_ref.shape[1]
    qi = pl.program_id(2)

    @pl.when(qi == 0)
    def _():
        vx_ref[:, :LANES] = v_ref[0]
        vx_ref[:, LANES:] = jnp.ones((S, LANES), BF16)

    q = q_ref[0]
    lane = lax.broadcasted_iota(jnp.int32, (tq, LANES), 1)
    first = lane < HEAD_DIM
    zero = jnp.zeros_like(q)
    q2 = jnp.concatenate([jnp.where(first, q, zero), jnp.where(first, zero, q)], axis=0)
    n_strip = strip_ref.shape[1]
    kc = ATT_KC
    m = None
    for c in range(S // kc):
        sc = lax.dot_general(q2, k_ref[0, c * kc:(c + 1) * kc, :], (((1,), (1,)), ((), ())),
                             preferred_element_type=F32)
        off = pl.multiple_of(S - qi * tq + c * kc, LANES)
        bias = [strip_ref[0, e % n_strip, :, pl.ds(off, kc)] for e in range(n_strip)]
        sc = sc + jnp.concatenate([bias[0], bias[-1]], axis=0)
        s_ref[:, c * kc:(c + 1) * kc] = sc
        mc = jnp.max(sc, axis=-1, keepdims=True)
        m = mc if m is None else jnp.maximum(m, mc)
    ox = [None, None]
    for c in range(S // kc):
        p = jnp.exp2(s_ref[:, c * kc:(c + 1) * kc] - m).astype(BF16)
        for e in range(2):
            part = jnp.dot(p[e * tq:(e + 1) * tq], vx_ref[c * kc:(c + 1) * kc, :],
                           preferred_element_type=F32)
            ox[e] = part if ox[e] is None else ox[e] + part
    outs = [ox[e][:, :LANES] / ox[e][:, LANES:] for e in range(2)]
    if mode == "diff":
        lv = lamv_ref[...]
        lam = (jnp.exp(jnp.sum(lv[0:1] * lv[1:2], axis=-1, keepdims=True))
               - jnp.exp(jnp.sum(lv[2:3] * lv[3:4], axis=-1, keepdims=True)) + lam_init)
        o = outs[0] - lam * outs[1]
        o = _rms(o, subln_ref[...]) * (1.0 - lam_init)
    else:
        o = jnp.where(first, outs[0], outs[1])
    o_ref[0] = o.astype(o_ref.dtype)


def attention_pairs(zatt, strips, lamv, subln, *, mode, lam_init, q_blk, k_blk, v_blk):
    B, S, _ = zatt.shape
    n_pairs = strips.shape[0]
    tq = strips.shape[2]
    return pl.pallas_call(
        functools.partial(_attn_kernel, mode=mode, lam_init=lam_init),
        grid=(n_pairs, B, S // tq),
        in_specs=[
            pl.BlockSpec((1, tq, LANES), lambda p, b, i: (b, i, q_blk + p)),
            pl.BlockSpec((1, S, LANES), lambda p, b, i: (b, 0, k_blk + p)),
            pl.BlockSpec((1, S, LANES), lambda p, b, i: (b, 0, v_blk + p)),
            pl.BlockSpec((1, strips.shape[1], tq, 2 * S), lambda p, b, i: (p, 0, 0, 0),
                         pipeline_mode=pl.Buffered(1)),
            _resident(lamv.shape),
            _resident(subln.shape),
        ],
        out_specs=pl.BlockSpec((1, tq, LANES), lambda p, b, i: (b, i, p)),
        out_shape=jax.ShapeDtypeStruct((B, S, n_pairs * LANES), BF16),
        scratch_shapes=[pltpu.VMEM((S, 2 * LANES), BF16), pltpu.VMEM((2 * tq, S), F32)],
        compiler_params=_cparams(("parallel", "parallel", "arbitrary")),
        name="attn_" + mode,
    )(zatt, zatt, zatt, strips, lamv, subln)


def _alibi_slopes():
    h = np.arange(1, N_ALIBI + 1, dtype=np.float64)
    s = np.exp2(-8.0 * h / N_ALIBI)
    return s[list(ALIBI_A_IDX)], s[list(ALIBI_B_IDX)]


def _bias_strips(S, tq):
    sl_a, sl_b = _alibi_slopes()
    r = np.arange(tq, dtype=np.float64)[:, None]
    c = np.arange(2 * S, dtype=np.float64)[None, :]
    delta = np.abs(r - (c - S))
    strips_a = np.stack([(-s * delta)[None] for s in sl_a])
    mult = np.zeros_like(delta)
    for window, dilation in DIL_PATTERNS:
        reach = (window // (2 * dilation)) * dilation
        mult += ((delta % dilation) == 0) & (delta <= reach)
    logm = np.where(mult > 0, np.log(np.maximum(mult, 1.0)), NEG_BIG)
    strips_b = np.stack([np.stack([-sl_b[2 * p] * delta + logm, -sl_b[2 * p + 1] * delta + logm])
                         for p in range(DIL_HEADS // 2)])
    return jnp.asarray(strips_a * LOG2E, F32), jnp.asarray(strips_b * LOG2E, F32)


def _hgrn_tables(C):
    levels = int(np.log2(C))
    t = np.arange(C)
    tri = np.stack([t[None, :] <= t[:, None], t[None, :] >= t[:, None]]).astype(np.float32)
    level = -np.ones((2, C, C), np.int32)
    for l in range(levels):
        m = 1 << l
        upper = (t % (2 * m)) >= m
        same = (t[:, None] // (2 * m)) == (t[None, :] // (2 * m))
        level[0][same & upper[:, None] & (~upper)[None, :]] = l
        level[1][same & (~upper)[:, None] & upper[None, :]] = l
    return tri, level


def _hgrn_kernel(q_ref, ff_ref, fb_ref, i_ref, og_ref, lbp_ref, ng_ref, tri_ref, level_ref,
                 o_ref, acc_ref, st_ref, *, layer):
    S = q_ref.shape[1]
    C = HGRN_C
    n_chunks = S // C
    levels = int(np.log2(C))
    row = lax.broadcasted_iota(jnp.int32, (C, LANES), 0)

    lbp = lbp_ref[...]
    e = jnp.exp(lbp - jnp.max(lbp, axis=0, keepdims=True))
    pr = e / jnp.sum(e, axis=0, keepdims=True)
    lb = jnp.zeros((1, LANES), F32)
    for j in range(1, layer + 1):
        lb = lb + pr[j:j + 1]

    st_ref[...] = jnp.zeros_like(st_ref)

    def one_direction(d, rows, q, v, vt_b, v_b, f_ref):
        x = f_ref[0, rows, :]
        f = lb + (1.0 - lb) * jax.nn.sigmoid(x)
        g = jnp.log(f)
        kk = 1.0 - f
        g0 = g.astype(BF16)
        r1 = g - g0.astype(F32)
        g1 = r1.astype(BF16)
        g2 = (r1 - g1.astype(F32)).astype(BF16)
        tri = tri_ref[d]
        run = (jnp.dot(tri, g0, preferred_element_type=F32)
               + jnp.dot(tri, g1, preferred_element_type=F32)
               + jnp.dot(tri, g2, preferred_element_type=F32))
        lev = level_ref[d]
        a = jnp.zeros((C, C), F32)
        edge = run
        for l in range(levels):
            m = 1 << l
            upper = (row & m) != 0
            dn = pltpu.roll(edge, m, 0)
            up = pltpu.roll(edge, C - m, 0)
            if d == 0:
                mid = jnp.where(upper, dn, edge)
                edge = jnp.where(upper, edge, up)
            else:
                mid = jnp.where(upper, edge, up)
                edge = jnp.where(upper, dn, edge)
            ex = jnp.exp(-jnp.abs(run - mid))
            r = lax.dot_general((q * ex).astype(BF16), (kk * ex).astype(BF16),
                                (((1,), (1,)), ((), ())), preferred_element_type=F32)
            a = jnp.where(lev == l, r, a)
        tot = run[C - 1:C] if d == 0 else run[0:1]
        st = st_ref[d]
        qe = (q * jnp.exp(run)).astype(BF16)
        o = lax.dot_general(qe, st.astype(BF16), (((1,), (1,)), ((), ())),
                            preferred_element_type=F32)
        o = o + jnp.dot(a.astype(BF16), v_b, preferred_element_type=F32)
        o = o + jnp.sum(q * kk, axis=-1, keepdims=True) * v
        kh = (kk * jnp.exp(tot - run)).astype(BF16)
        st_ref[d] = st * jnp.exp(tot) + jnp.dot(vt_b, kh, preferred_element_type=F32)
        return o

    def body(c, carry):
        for d in range(2):
            cc = c if d == 0 else n_chunks - 1 - c
            rows = pl.ds(pl.multiple_of(cc * C, C), C)
            q = q_ref[0, rows, :]
            v = i_ref[0, rows, :]
            v_b = v.astype(BF16)
            vt_b = v.T.astype(BF16)
            o = one_direction(d, rows, q, v, vt_b, v_b, ff_ref if d == 0 else fb_ref)
            acc_ref[d, rows, :] = o
        return carry

    lax.fori_loop(0, n_chunks, body, 0, unroll=2)
    og = og_ref[0]
    o = _rms(acc_ref[0] + acc_ref[1], ng_ref[...]) * (og * jax.nn.sigmoid(og))
    o_ref[0] = o.astype(o_ref.dtype)


def hgrn_mixer(zhg, lower_bounds, norm_g, layer):
    B, S, _ = zhg.shape
    tri, level = _hgrn_tables(HGRN_C)
    tri = jnp.asarray(tri, BF16)
    level = jnp.asarray(level)
    depth = lower_bounds.shape[0]
    col = lambda g: pl.BlockSpec((1, S, LANES), lambda b, h: (b, 0, g * HGRN_HEADS + h))
    return pl.pallas_call(
        functools.partial(_hgrn_kernel, layer=layer),
        grid=(B, HGRN_HEADS),
        in_specs=[col(0), col(1), col(2), col(3), col(4),
                  pl.BlockSpec((depth, LANES), lambda b, h: (0, h)),
                  _resident((1, LANES)), _resident(tri.shape), _resident(level.shape)],
        out_specs=pl.BlockSpec((1, S, LANES), lambda b, h: (b, 0, h)),
        out_shape=jax.ShapeDtypeStruct((B, S, HGRN_HEADS * LANES), BF16),
        scratch_shapes=[pltpu.VMEM((2, S, LANES), F32), pltpu.VMEM((2, LANES, LANES), F32)],
        compiler_params=_cparams(("parallel", "parallel")),
        name="hgrn",
    )(zhg, zhg, zhg, zhg, zhg, lower_bounds, norm_g.reshape(1, LANES), tri, level)


def _merge_kernel(h_ref, ya_ref, yb_ref, yc_ref, g_ref, wa_ref, wb_ref, wc_ref, wo_ref, o_ref):
    D = h_ref.shape[1]
    merged = (g_ref[:, 0:D] * jnp.dot(ya_ref[...], wa_ref[...], preferred_element_type=F32)
              + g_ref[:, D:2 * D] * jnp.dot(yb_ref[...], wb_ref[...], preferred_element_type=F32)
              + g_ref[:, 2 * D:3 * D] * jnp.dot(yc_ref[...], wc_ref[...], preferred_element_type=F32))
    o_ref[...] = h_ref[...] + jnp.dot(merged.astype(BF16), wo_ref[...], preferred_element_type=F32)


def merge_block(h, ya, yb, yc, gates, wa, wb, wc, wo):
    T, D = h.shape
    tm = min(FFN_TM, T)
    row = lambda w: pl.BlockSpec((tm, w), lambda i: (i, 0))
    return pl.pallas_call(
        _merge_kernel,
        grid=(T // tm,),
        in_specs=[row(D), row(ya.shape[1]), row(yb.shape[1]), row(yc.shape[1]), row(gates.shape[1]),
                  _resident(wa.shape), _resident(wb.shape), _resident(wc.shape), _resident(wo.shape)],
        out_specs=row(D),
        out_shape=jax.ShapeDtypeStruct((T, D), F32),
        compiler_params=_cparams(("parallel",)),
        name="merge",
    )(h, ya, yb, yc, gates, wa, wb, wc, wo)


def kernel(x, ffn1_norm, ffn1_w_gate, ffn1_w_up, ffn1_w_down, mix_norm, w_in, diff_lambda_q1, diff_lambda_k1, diff_lambda_q2, diff_lambda_k2, diff_subln, hgrn_lower_bounds, hgrn_norm, w_branch_a, w_branch_b, w_branch_c, w_out, ffn2_norm, ffn2_w_gate, ffn2_w_up, ffn2_w_down, final_norm):
    B, S, D = x.shape
    T = B * S
    depth = w_in.shape[0]
    bf = lambda w: w.astype(BF16)
    strips_a, strips_b = _bias_strips(S, min(ATT_TQ, S))
    col = np.arange(w_in.shape[2])
    is_q = (col < 512) | ((col >= 3 * 512) & (col < 4 * 512))
    col_scale = jnp.asarray(np.where(is_q, LOG2E * HEAD_DIM ** -0.5, 1.0), F32)
    h = x.reshape(T, D)
    for l in range(depth):
        h, u = ffn_block(h, ffn1_norm[l], bf(ffn1_w_gate[l]), bf(ffn1_w_up[l]), bf(ffn1_w_down[l]),
                         mix_norm[l], "hn", BF16)
        w = bf(w_in[l] * col_scale)
        zatt = project(u, w[:, :ATT_W], BF16, tn=ATT_W // 2).reshape(B, S, ATT_W)
        zhg = project(u, w[:, ATT_W:ATT_W + HG_W], F32, tn=HG_W // 2).reshape(B, S, HG_W)
        gates = project(u, w[:, ATT_W + HG_W:], F32, tn=GATE_W // 2, act="sigmoid")
        lam_init = 0.8 - 0.6 * float(np.exp(-0.3 * l))
        lamv = jnp.stack([diff_lambda_q1[l], diff_lambda_k1[l], diff_lambda_q2[l], diff_lambda_k2[l]])
        subln = diff_subln[l].reshape(1, 2 * HEAD_DIM)
        ya = attention_pairs(zatt, strips_a, lamv, subln, mode="diff", lam_init=lam_init,
                             q_blk=0, k_blk=4, v_blk=8)
        yb = attention_pairs(zatt, strips_b, lamv, subln, mode="dil", lam_init=lam_init,
                             q_blk=12, k_blk=16, v_blk=20)
        yc = hgrn_mixer(zhg, hgrn_lower_bounds, hgrn_norm[l], l)
        h = merge_block(h, ya.reshape(T, -1), yb.reshape(T, -1), yc.reshape(T, -1), gates,
                        bf(w_branch_a[l]), bf(w_branch_b[l]), bf(w_branch_c[l]), bf(w_out[l]))
        (h,) = ffn_block(h, ffn2_norm[l], bf(ffn2_w_gate[l]), bf(ffn2_w_up[l]), bf(ffn2_w_down[l]),
                         final_norm, "n" if l == depth - 1 else "h")
    return h.reshape(B, S, D)
```

```python
import functools

import numpy as np
import jax
import jax.numpy as jnp
from jax import lax
from jax.experimental import pallas as pl
from jax.experimental.pallas import tpu as pltpu

F32 = jnp.float32
BF16 = jnp.bfloat16

D_MODEL = 1024
HEAD_DIM = 64
DIFF_HEADS = 4
DIL_HEADS = 8
DIL_PATTERNS = ((128, 1), (512, 4), (2048, 16))
HGRN_HEADS = 4
HGRN_DK = 128
N_BRANCH = 3
D_FF = 2816
NORM_EPS = 1e-6
N_ALIBI = DIFF_HEADS + DIL_HEADS
ALIBI_A_IDX = (0, 3, 6, 9)
ALIBI_B_IDX = (1, 2, 4, 5, 7, 8, 10, 11)
ATT_W = 6 * 512
HG_W = 5 * 512
GATE_W = N_BRANCH * D_MODEL

LANES = 128
SUBLANE_LEVELS = 3
VMEM_LIMIT = 56 * 1024 * 1024

FFN_TM = 512
MM_TM = 1024
ATT_TQ = 512
ATT_KC = 512
HGRN_C = 128
HGRN_GROUP = 4
NEG_BIG = -1e30
LOG2E = float(np.log2(np.e))


def _cparams(sem):
    return pltpu.CompilerParams(dimension_semantics=sem, vmem_limit_bytes=VMEM_LIMIT)


def _resident(shape):
    nd = len(shape)
    return pl.BlockSpec(shape, lambda *_: (0,) * nd, pipeline_mode=pl.Buffered(1))


def _rms(x, g):
    return x * lax.rsqrt(jnp.mean(x * x, axis=-1, keepdims=True) + NORM_EPS) * g


def _ffn_kernel(h_ref, g1_ref, wg_ref, wu_ref, wd_ref, g2_ref, *out_refs, emit):
    h = h_ref[...]
    n = _rms(h, g1_ref[...]).astype(BF16)
    a = jnp.dot(n, wg_ref[...], preferred_element_type=F32)
    b = jnp.dot(n, wu_ref[...], preferred_element_type=F32)
    t = (a * jax.nn.sigmoid(a) * b).astype(BF16)
    hout = h + 0.5 * jnp.dot(t, wd_ref[...], preferred_element_type=F32)
    if "h" in emit:
        out_refs[0][...] = hout
    if "n" in emit:
        out_refs[-1][...] = _rms(hout, g2_ref[...]).astype(out_refs[-1].dtype)


def ffn_block(h, g1, wg, wu, wd, g2, emit, n_dtype=F32):
    T, D = h.shape
    tm = min(FFN_TM, T)
    row = pl.BlockSpec((tm, D), lambda i: (i, 0))
    out_dtypes = [F32] * ("h" in emit) + [n_dtype] * ("n" in emit)
    return pl.pallas_call(
        functools.partial(_ffn_kernel, emit=emit),
        grid=(T // tm,),
        in_specs=[row, _resident((1, D)), _resident(wg.shape), _resident(wu.shape),
                  _resident(wd.shape), _resident((1, D))],
        out_specs=[row] * len(out_dtypes),
        out_shape=[jax.ShapeDtypeStruct((T, D), dt) for dt in out_dtypes],
        compiler_params=_cparams(("parallel",)),
        name="ffn",
    )(h, g1.reshape(1, D), wg, wu, wd, g2.reshape(1, D))


def _proj_kernel(x_ref, w_ref, o_ref, *, act):
    acc = jnp.dot(x_ref[...], w_ref[...], preferred_element_type=F32)
    if act == "sigmoid":
        acc = jax.nn.sigmoid(acc)
    o_ref[...] = acc.astype(o_ref.dtype)


def project(x, w, out_dtype, tn, act=None):
    T, K = x.shape
    N = w.shape[1]
    tm = min(MM_TM, T)
    return pl.pallas_call(
        functools.partial(_proj_kernel, act=act),
        grid=(T // tm, N // tn),
        in_specs=[pl.BlockSpec((tm, K), lambda i, j: (i, 0)),
                  pl.BlockSpec((K, tn), lambda i, j: (0, j))],
        out_specs=pl.BlockSpec((tm, tn), lambda i, j: (i, j)),
        out_shape=jax.ShapeDtypeStruct((T, N), out_dtype),
        compiler_params=_cparams(("parallel", "arbitrary")),
        name="proj_in",
    )(x, w)


def _attn_kernel(q_ref, k_ref, v_ref, strip_ref, lamv_ref, subln_ref, o_ref, vx_ref, s_ref, *,
                 mode, lam_init):
    tq = q_ref.shape[1]
    S = k_ref.shape[1]
    qi = pl.program_id(2)

    @pl.when(qi == 0)
    def _():
        vx_ref[:, :LANES] = v_ref[0]
        vx_ref[:, LANES:] = jnp.ones((S, LANES), BF16)

    q = q_ref[0]
    lane = lax.broadcasted_iota(jnp.int32, (tq, LANES), 1)
    first = lane < HEAD_DIM
    zero = jnp.zeros_like(q)
    q2 = jnp.concatenate([jnp.where(first, q, zero), jnp.where(first, zero, q)], axis=0)
    n_strip = strip_ref.shape[1]
    kc = ATT_KC
    m = None
    for c in range(S // kc):
        sc = lax.dot_general(q2, k_ref[0, c * kc:(c + 1) * kc, :], (((1,), (1,)), ((), ())),
                             preferred_element_type=F32)
        off = pl.multiple_of(S - qi * tq + c * kc, LANES)
        bias = [strip_ref[0, e % n_strip, :, pl.ds(off, kc)] for e in range(n_strip)]
        sc = sc + jnp.concatenate([bias[0], bias[-1]], axis=0)
        s_ref[:, c * kc:(c + 1) * kc] = sc
        mc = jnp.max(sc, axis=-1, keepdims=True)
        m = mc if m is None else jnp.maximum(m, mc)
    ox = [None, None]
    for c in range(S // kc):
        p = jnp.exp2(s_ref[:, c * kc:(c + 1) * kc] - m).astype(BF16)
        for e in range(2):
            part = jnp.dot(p[e * tq:(e + 1) * tq], vx_ref[c * kc:(c + 1) * kc, :],
                           preferred_element_type=F32)
            ox[e] = part if ox[e] is None else ox[e] + part
    outs = [ox[e][:, :LANES] / ox[e][:, LANES:] for e in range(2)]
    if mode == "diff":
        lv = lamv_ref[...]
        lam = (jnp.exp(jnp.sum(lv[0:1] * lv[1:2], axis=-1, keepdims=True))
               - jnp.exp(jnp.sum(lv[2:3] * lv[3:4], axis=-1, keepdims=True)) + lam_init)
        o = outs[0] - lam * outs[1]
        o = _rms(o, subln_ref[...]) * (1.0 - lam_init)
    else:
        o = jnp.where(first, outs[0], outs[1])
    o_ref[0] = o.astype(o_ref.dtype)


def attention_pairs(zatt, strips, lamv, subln, *, mode, lam_init, q_blk, k_blk, v_blk):
    B, S, _ = zatt.shape
    n_pairs = strips.shape[0]
    tq = strips.shape[2]
    return pl.pallas_call(
        functools.partial(_attn_kernel, mode=mode, lam_init=lam_init),
        grid=(n_pairs, B, S // tq),
        in_specs=[
            pl.BlockSpec((1, tq, LANES), lambda p, b, i: (b, i, q_blk + p)),
            pl.BlockSpec((1, S, LANES), lambda p, b, i: (b, 0, k_blk + p)),
            pl.BlockSpec((1, S, LANES), lambda p, b, i: (b, 0, v_blk + p)),
            pl.BlockSpec((1, strips.shape[1], tq, 2 * S), lambda p, b, i: (p, 0, 0, 0),
                         pipeline_mode=pl.Buffered(1)),
            _resident(lamv.shape),
            _resident(subln.shape),
        ],
        out_specs=pl.BlockSpec((1, tq, LANES), lambda p, b, i: (b, i, p)),
        out_shape=jax.ShapeDtypeStruct((B, S, n_pairs * LANES), BF16),
        scratch_shapes=[pltpu.VMEM((S, 2 * LANES), BF16), pltpu.VMEM((2 * tq, S), F32)],
        compiler_params=_cparams(("parallel", "parallel", "arbitrary")),
        name="attn_" + mode,
    )(zatt, zatt, zatt, strips, lamv, subln)


def _alibi_slopes():
    h = np.arange(1, N_ALIBI + 1, dtype=np.float64)
    s = np.exp2(-8.0 * h / N_ALIBI)
    return s[list(ALIBI_A_IDX)], s[list(ALIBI_B_IDX)]


def _bias_strips(S, tq):
    sl_a, sl_b = _alibi_slopes()
    r = np.arange(tq, dtype=np.float64)[:, None]
    c = np.arange(2 * S, dtype=np.float64)[None, :]
    delta = np.abs(r - (c - S))
    strips_a = np.stack([(-s * delta)[None] for s in sl_a])
    mult = np.zeros_like(delta)
    for window, dilation in DIL_PATTERNS:
        reach = (window // (2 * dilation)) * dilation
        mult += ((delta % dilation) == 0) & (delta <= reach)
    logm = np.where(mult > 0, np.log(np.maximum(mult, 1.0)), NEG_BIG)
    strips_b = np.stack([np.stack([-sl_b[2 * p] * delta + logm, -sl_b[2 * p + 1] * delta + logm])
                         for p in range(DIL_HEADS // 2)])
    return jnp.asarray(strips_a * LOG2E, F32), jnp.asarray(strips_b * LOG2E, F32)


def _hgrn_tables(C):
    levels = int(np.log2(C))
    t = np.arange(C)
    tri = np.stack([t[None, :] <= t[:, None], t[None, :] >= t[:, None]]).astype(np.float32)
    level = -np.ones((2, C, C), np.int32)
    for l in range(levels):
        m = 1 << l
        upper = (t % (2 * m)) >= m
        same = (t[:, None] // (2 * m)) == (t[None, :] // (2 * m))
        level[0][same & upper[:, None] & (~upper)[None, :]] = l
        level[1][same & (~upper)[:, None] & upper[None, :]] = l
    return tri, level


def _hgrn_kernel(q_ref, ff_ref, fb_ref, i_ref, og_ref, lbp_ref, ng_ref, tri_ref, level_ref,
                 o_ref, acc_ref, st_ref, *, layer):
    S = q_ref.shape[1]
    C = HGRN_C
    n_chunks = S // C
    levels = int(np.log2(C))
    row = lax.broadcasted_iota(jnp.int32, (C, LANES), 0)

    lbp = lbp_ref[...]
    e = jnp.exp(lbp - jnp.max(lbp, axis=0, keepdims=True))
    pr = e / jnp.sum(e, axis=0, keepdims=True)
    lb = jnp.zeros((1, LANES), F32)
    for j in range(1, layer + 1):
        lb = lb + pr[j:j + 1]

    st_ref[...] = jnp.zeros_like(st_ref)

    nt = (((1,), (1,)), ((), ()))

    def gates(it):
        x = (ff_ref if it["d"] == 0 else fb_ref)[0, it["rows"], :]
        f = lb + (1.0 - lb) * jax.nn.sigmoid(x)
        g = jnp.log(f)
        it["kk"] = 1.0 - f
        g0 = g.astype(BF16)
        r1 = g - g0.astype(F32)
        g1 = r1.astype(BF16)
        g2 = (r1 - g1.astype(F32)).astype(BF16)
        tri = tri_ref[it["d"]]
        it["run"] = (jnp.dot(tri, g0, preferred_element_type=F32)
                     + jnp.dot(tri, g1, preferred_element_type=F32)
                     + jnp.dot(tri, g2, preferred_element_type=F32))
        it["edge"] = it["run"]
        it["a"] = None

    def block_level(it, l):
        d, q, kk, run = it["d"], it["q"], it["kk"], it["run"]
        m = 1 << l
        zeros = jnp.zeros((m, LANES), F32)
        bnd = m - 1 if d == 0 else m
        qparts, kparts = [], []
        for s in range(0, C, 2 * m):
            lo, hi = slice(s, s + m), slice(s + m, s + 2 * m)
            mid = run[s + bnd:s + bnd + 1]
            ex_lo = jnp.exp2(jnp.abs(run[lo] - mid) * -LOG2E)
            ex_hi = jnp.exp2(jnp.abs(run[hi] - mid) * -LOG2E)
            if d == 0:
                qparts += [zeros, q[hi] * ex_hi]
                kparts += [kk[lo] * ex_lo, zeros]
            else:
                qparts += [q[lo] * ex_lo, zeros]
                kparts += [zeros, kk[hi] * ex_hi]
        r = lax.dot_general(jnp.concatenate(qparts, axis=0).astype(BF16),
                            jnp.concatenate(kparts, axis=0).astype(BF16), nt,
                            preferred_element_type=F32)
        it["a"] = r if it["a"] is None else jnp.where(level_ref[d] == l, r, it["a"])

    def sublane_level(it, l):
        d, edge = it["d"], it["edge"]
        m = 1 << l
        upper = (row & m) != 0
        dn = pltpu.roll(edge, m, 0)
        up = pltpu.roll(edge, C - m, 0)
        if d == 0:
            mid = jnp.where(upper, dn, edge)
            it["edge"] = jnp.where(upper, edge, up)
        else:
            mid = jnp.where(upper, edge, up)
            it["edge"] = jnp.where(upper, dn, edge)
        ex = jnp.exp2(jnp.abs(it["run"] - mid) * -LOG2E)
        r = lax.dot_general((it["q"] * ex).astype(BF16), (it["kk"] * ex).astype(BF16), nt,
                            preferred_element_type=F32)
        it["a"] = jnp.where(level_ref[d] == l, r, it["a"])

    def finish(it):
        d, q, kk, run, v = it["d"], it["q"], it["kk"], it["run"], it["v"]
        tot = run[C - 1:C] if d == 0 else run[0:1]
        st = st_ref[d]
        qe = (q * jnp.exp(run)).astype(BF16)
        o = lax.dot_general(qe, st.astype(BF16), nt, preferred_element_type=F32)
        o = o + jnp.dot(it["a"].astype(BF16), v.astype(BF16), preferred_element_type=F32)
        o = o + jnp.sum(q * kk, axis=-1, keepdims=True) * v
        kh = (kk * jnp.exp(tot - run)).astype(BF16)
        st_ref[d] = st * jnp.exp(tot) + jnp.dot(it["vt"], kh, preferred_element_type=F32)
        acc_ref[d, it["rows"], :] = o

    def body(c, carry):
        items = []
        for u in range(HGRN_GROUP):
            cf = c * HGRN_GROUP + u
            for d in range(2):
                cc = cf if d == 0 else n_chunks - 1 - cf
                rows = pl.ds(pl.multiple_of(cc * C, C), C)
                v = i_ref[0, rows, :]
                items.append(dict(d=d, rows=rows, q=q_ref[0, rows, :], v=v, vt=v.T.astype(BF16)))
        for it in items:
            gates(it)
        for l in range(levels - 1, SUBLANE_LEVELS - 1, -1):
            for it in items:
                block_level(it, l)
        for l in range(SUBLANE_LEVELS):
            for it in items:
                sublane_level(it, l)
        for it in items:
            finish(it)
        return carry

    lax.fori_loop(0, n_chunks // HGRN_GROUP, body, 0)
    og = og_ref[0]
    o = _rms(acc_ref[0] + acc_ref[1], ng_ref[...]) * (og * jax.nn.sigmoid(og))
    o_ref[0] = o.astype(o_ref.dtype)


def hgrn_mixer(zhg, lower_bounds, norm_g, layer):
    B, S, _ = zhg.shape
    tri, level = _hgrn_tables(HGRN_C)
    tri = jnp.asarray(tri, BF16)
    level = jnp.asarray(level)
    depth = lower_bounds.shape[0]
    col = lambda g: pl.BlockSpec((1, S, LANES), lambda b, h: (b, 0, g * HGRN_HEADS + h))
    return pl.pallas_call(
        functools.partial(_hgrn_kernel, layer=layer),
        grid=(B, HGRN_HEADS),
        in_specs=[col(0), col(1), col(2), col(3), col(4),
                  pl.BlockSpec((depth, LANES), lambda b, h: (0, h)),
                  _resident((1, LANES)), _resident(tri.shape), _resident(level.shape)],
        out_specs=pl.BlockSpec((1, S, LANES), lambda b, h: (b, 0, h)),
        out_shape=jax.ShapeDtypeStruct((B, S, HGRN_HEADS * LANES), BF16),
        scratch_shapes=[pltpu.VMEM((2, S, LANES), F32), pltpu.VMEM((2, LANES, LANES), F32)],
        compiler_params=_cparams(("parallel", "parallel")),
        name="hgrn",
    )(zhg, zhg, zhg, zhg, zhg, lower_bounds, norm_g.reshape(1, LANES), tri, level)


def _merge_kernel(h_ref, ya_ref, yb_ref, yc_ref, g_ref, wa_ref, wb_ref, wc_ref, wo_ref, o_ref):
    D = h_ref.shape[1]
    merged = (g_ref[:, 0:D] * jnp.dot(ya_ref[...], wa_ref[...], preferred_element_type=F32)
              + g_ref[:, D:2 * D] * jnp.dot(yb_ref[...], wb_ref[...], preferred_element_type=F32)
              + g_ref[:, 2 * D:3 * D] * jnp.dot(yc_ref[...], wc_ref[...], preferred_element_type=F32))
    o_ref[...] = h_ref[...] + jnp.dot(merged.astype(BF16), wo_ref[...], preferred_element_type=F32)


def merge_block(h, ya, yb, yc, gates, wa, wb, wc, wo):
    T, D = h.shape
    tm = min(FFN_TM, T)
    row = lambda w: pl.BlockSpec((tm, w), lambda i: (i, 0))
    return pl.pallas_call(
        _merge_kernel,
        grid=(T // tm,),
        in_specs=[row(D), row(ya.shape[1]), row(yb.shape[1]), row(yc.shape[1]), row(gates.shape[1]),
                  _resident(wa.shape), _resident(wb.shape), _resident(wc.shape), _resident(wo.shape)],
        out_specs=row(D),
        out_shape=jax.ShapeDtypeStruct((T, D), F32),
        compiler_params=_cparams(("parallel",)),
        name="merge",
    )(h, ya, yb, yc, gates, wa, wb, wc, wo)


def kernel(x, ffn1_norm, ffn1_w_gate, ffn1_w_up, ffn1_w_down, mix_norm, w_in, diff_lambda_q1, diff_lambda_k1, diff_lambda_q2, diff_lambda_k2, diff_subln, hgrn_lower_bounds, hgrn_norm, w_branch_a, w_branch_b, w_branch_c, w_out, ffn2_norm, ffn2_w_gate, ffn2_w_up, ffn2_w_down, final_norm):
    B, S, D = x.shape
    T = B * S
    depth = w_in.shape[0]
    bf = lambda w: w.astype(BF16)
    strips_a, strips_b = _bias_strips(S, min(ATT_TQ, S))
    col = np.arange(w_in.shape[2])
    is_q = (col < 512) | ((col >= 3 * 512) & (col < 4 * 512))
    col_scale = jnp.asarray(np.where(is_q, LOG2E * HEAD_DIM ** -0.5, 1.0), F32)
    h = x.reshape(T, D)
    for l in range(depth):
        h, u = ffn_block(h, ffn1_norm[l], bf(ffn1_w_gate[l]), bf(ffn1_w_up[l]), bf(ffn1_w_down[l]),
                         mix_norm[l], "hn", BF16)
        w = bf(w_in[l] * col_scale)
        zatt = project(u, w[:, :ATT_W], BF16, tn=ATT_W // 2).reshape(B, S, ATT_W)
        zhg = project(u, w[:, ATT_W:ATT_W + HG_W], F32, tn=HG_W // 2).reshape(B, S, HG_W)
        gates = project(u, w[:, ATT_W + HG_W:], BF16, tn=GATE_W // 2, act="sigmoid")
        lam_init = 0.8 - 0.6 * float(np.exp(-0.3 * l))
        lamv = jnp.stack([diff_lambda_q1[l], diff_lambda_k1[l], diff_lambda_q2[l], diff_lambda_k2[l]])
        subln = diff_subln[l].reshape(1, 2 * HEAD_DIM)
        ya = attention_pairs(zatt, strips_a, lamv, subln, mode="diff", lam_init=lam_init,
                             q_blk=0, k_blk=4, v_blk=8)
        yb = attention_pairs(zatt, strips_b, lamv, subln, mode="dil", lam_init=lam_init,
                             q_blk=12, k_blk=16, v_blk=20)
        yc = hgrn_mixer(zhg, hgrn_lower_bounds, hgrn_norm[l], l)
        h = merge_block(h, ya.reshape(T, -1), yb.reshape(T, -1), yc.reshape(T, -1), gates,
                        bf(w_branch_a[l]), bf(w_branch_b[l]), bf(w_branch_c[l]), bf(w_out[l]))
        (h,) = ffn_block(h, ffn2_norm[l], bf(ffn2_w_gate[l]), bf(ffn2_w_up[l]), bf(ffn2_w_down[l]),
                         final_norm, "n" if l == depth - 1 else "h")
    return h.reshape(B, S, D)
```

```python
import functools

import numpy as np
import jax
import jax.numpy as jnp
from jax import lax
from jax.experimental import pallas as pl
from jax.experimental.pallas import tpu as pltpu

F32 = jnp.float32
BF16 = jnp.bfloat16

D_MODEL = 1024
HEAD_DIM = 64
DIFF_HEADS = 4
DIL_HEADS = 8
DIL_PATTERNS = ((128, 1), (512, 4), (2048, 16))
HGRN_HEADS = 4
HGRN_DK = 128
N_BRANCH = 3
D_FF = 2816
NORM_EPS = 1e-6
N_ALIBI = DIFF_HEADS + DIL_HEADS
ALIBI_A_IDX = (0, 3, 6, 9)
ALIBI_B_IDX = (1, 2, 4, 5, 7, 8, 10, 11)
ATT_W = 6 * 512
HG_W = 5 * 512

LANES = 128
SUBLANE_LEVELS = 3
VMEM_LIMIT = 56 * 1024 * 1024

FFN_TM = 512
MM_TM = 1024
ATT_TQ = 512
ATT_TILES = {"diff": 4, "dil": 2}
ATT_KC = 512
HGRN_C = 128
HGRN_GROUP = 8
NEG_BIG = -1e30
LOG2E = float(np.log2(np.e))


def _cparams(sem):
    return pltpu.CompilerParams(dimension_semantics=sem, vmem_limit_bytes=VMEM_LIMIT)


def _resident(shape):
    nd = len(shape)
    return pl.BlockSpec(shape, lambda *_: (0,) * nd, pipeline_mode=pl.Buffered(1))


def _rms(x, g):
    return x * lax.rsqrt(jnp.mean(x * x, axis=-1, keepdims=True) + NORM_EPS) * g


def _ffn_kernel(h_ref, g1_ref, wg_ref, wu_ref, wd_ref, g2_ref, *out_refs, emit):
    h = h_ref[...]
    n = _rms(h, g1_ref[...]).astype(BF16)
    a = jnp.dot(n, wg_ref[...], preferred_element_type=F32)
    b = jnp.dot(n, wu_ref[...], preferred_element_type=F32)
    t = (a * jax.nn.sigmoid(a) * b).astype(BF16)
    hout = h + 0.5 * jnp.dot(t, wd_ref[...], preferred_element_type=F32)
    if "h" in emit:
        out_refs[0][...] = hout
    if "n" in emit:
        out_refs[-1][...] = _rms(hout, g2_ref[...]).astype(out_refs[-1].dtype)


def ffn_block(h, g1, wg, wu, wd, g2, emit, n_dtype=F32):
    T, D = h.shape
    tm = min(FFN_TM, T)
    row = pl.BlockSpec((tm, D), lambda i: (i, 0))
    out_dtypes = [F32] * ("h" in emit) + [n_dtype] * ("n" in emit)
    return pl.pallas_call(
        functools.partial(_ffn_kernel, emit=emit),
        grid=(T // tm,),
        in_specs=[row, _resident((1, D)), _resident(wg.shape), _resident(wu.shape),
                  _resident(wd.shape), _resident((1, D))],
        out_specs=[row] * len(out_dtypes),
        out_shape=[jax.ShapeDtypeStruct((T, D), dt) for dt in out_dtypes],
        compiler_params=_cparams(("parallel",)),
        name="ffn",
    )(h, g1.reshape(1, D), wg, wu, wd, g2.reshape(1, D))


def _proj_kernel(x_ref, w_ref, o_ref):
    o_ref[...] = jnp.dot(x_ref[...], w_ref[...], preferred_element_type=F32).astype(o_ref.dtype)


def project(x, w, out_dtype, tn):
    T, K = x.shape
    N = w.shape[1]
    tm = min(MM_TM, T)
    return pl.pallas_call(
        _proj_kernel,
        grid=(T // tm, N // tn),
        in_specs=[pl.BlockSpec((tm, K), lambda i, j: (i, 0)),
                  pl.BlockSpec((K, tn), lambda i, j: (0, j))],
        out_specs=pl.BlockSpec((tm, tn), lambda i, j: (i, j)),
        out_shape=jax.ShapeDtypeStruct((T, N), out_dtype),
        compiler_params=_cparams(("parallel", "arbitrary")),
        name="proj_in",
    )(x, w)


def _attn_kernel(q_ref, k_ref, v_ref, strip_ref, lamv_ref, subln_ref, o_ref, vx_ref, s_ref, *,
                 mode, lam_init):
    tq = strip_ref.shape[2]
    n_tiles = q_ref.shape[1] // tq
    S = k_ref.shape[1]
    qi = pl.program_id(2)

    @pl.when(qi == 0)
    def _():
        vx_ref[:, :LANES] = v_ref[0]
        vx_ref[:, LANES:] = jnp.ones((S, LANES), BF16)

    lane = lax.broadcasted_iota(jnp.int32, (tq, LANES), 1)
    first = lane < HEAD_DIM
    n_strip = strip_ref.shape[1]
    kc = ATT_KC
    n_kc = S // kc
    if mode == "diff":
        lv = lamv_ref[...]
        lam = (jnp.exp(jnp.sum(lv[0:1] * lv[1:2], axis=-1, keepdims=True))
               - jnp.exp(jnp.sum(lv[2:3] * lv[3:4], axis=-1, keepdims=True)) + lam_init)

    def stacked_q(t):
        q = q_ref[0, t * tq:(t + 1) * tq, :]
        zero = jnp.zeros_like(q)
        return jnp.concatenate([jnp.where(first, q, zero), jnp.where(first, zero, q)], axis=0)

    def score_chunk(t, q2, c, m):
        sc = lax.dot_general(q2, k_ref[0, c * kc:(c + 1) * kc, :], (((1,), (1,)), ((), ())),
                             preferred_element_type=F32)
        off = pl.multiple_of(S - (qi * n_tiles + t) * tq + c * kc, LANES)
        bias = [strip_ref[0, e % n_strip, :, pl.ds(off, kc)] for e in range(n_strip)]
        sc = sc + jnp.concatenate([bias[0], bias[-1]], axis=0)
        s_ref[t, :, c * kc:(c + 1) * kc] = sc
        mc = jnp.max(sc, axis=-1, keepdims=True)
        return mc if m is None else jnp.maximum(m, mc)

    def value_chunk(t, c, m, ox):
        p = jnp.exp2(s_ref[t, :, c * kc:(c + 1) * kc] - m).astype(BF16)
        for e in range(2):
            part = jnp.dot(p[e * tq:(e + 1) * tq], vx_ref[c * kc:(c + 1) * kc, :],
                           preferred_element_type=F32)
            ox[e] = part if ox[e] is None else ox[e] + part

    def finish(t, ox):
        outs = [ox[e][:, :LANES] / ox[e][:, LANES:] for e in range(2)]
        if mode == "diff":
            o = outs[0] - lam * outs[1]
            o = _rms(o, subln_ref[...]) * (1.0 - lam_init)
        else:
            o = jnp.where(first, outs[0], outs[1])
        o_ref[0, t * tq:(t + 1) * tq, :] = o.astype(o_ref.dtype)

    q2 = stacked_q(0)
    m = None
    for c in range(n_kc):
        m = score_chunk(0, q2, c, m)
    for t in range(n_tiles):
        ox = [None, None]
        if t + 1 < n_tiles:
            q2_next, m_next = stacked_q(t + 1), None
        for c in range(n_kc):
            value_chunk(t, c, m, ox)
            if t + 1 < n_tiles:
                m_next = score_chunk(t + 1, q2_next, c, m_next)
        finish(t, ox)
        if t + 1 < n_tiles:
            m = m_next


def attention_pairs(zatt, strips, lamv, subln, *, mode, lam_init, q_blk, k_blk, v_blk):
    B, S, _ = zatt.shape
    n_pairs = strips.shape[0]
    tq = strips.shape[2]
    n_tiles = min(ATT_TILES[mode], S // tq)
    rows = n_tiles * tq
    return pl.pallas_call(
        functools.partial(_attn_kernel, mode=mode, lam_init=lam_init),
        grid=(n_pairs, B, S // rows),
        in_specs=[
            pl.BlockSpec((1, rows, LANES), lambda p, b, i: (b, i, q_blk + p)),
            pl.BlockSpec((1, S, LANES), lambda p, b, i: (b, 0, k_blk + p)),
            pl.BlockSpec((1, S, LANES), lambda p, b, i: (b, 0, v_blk + p)),
            pl.BlockSpec((1, strips.shape[1], tq, 2 * S), lambda p, b, i: (p, 0, 0, 0),
                         pipeline_mode=pl.Buffered(1)),
            _resident(lamv.shape),
            _resident(subln.shape),
        ],
        out_specs=pl.BlockSpec((1, rows, LANES), lambda p, b, i: (b, i, p)),
        out_shape=jax.ShapeDtypeStruct((B, S, n_pairs * LANES), BF16),
        scratch_shapes=[pltpu.VMEM((S, 2 * LANES), BF16), pltpu.VMEM((n_tiles, 2 * tq, S), F32)],
        compiler_params=_cparams(("parallel", "parallel", "arbitrary")),
        name="attn_" + mode,
    )(zatt, zatt, zatt, strips, lamv, subln)


def _alibi_slopes():
    h = np.arange(1, N_ALIBI + 1, dtype=np.float64)
    s = np.exp2(-8.0 * h / N_ALIBI)
    return s[list(ALIBI_A_IDX)], s[list(ALIBI_B_IDX)]


def _bias_strips(S, tq):
    sl_a, sl_b = _alibi_slopes()
    r = np.arange(tq, dtype=np.float64)[:, None]
    c = np.arange(2 * S, dtype=np.float64)[None, :]
    delta = np.abs(r - (c - S))
    strips_a = np.stack([(-s * delta)[None] for s in sl_a])
    mult = np.zeros_like(delta)
    for window, dilation in DIL_PATTERNS:
        reach = (window // (2 * dilation)) * dilation
        mult += ((delta % dilation) == 0) & (delta <= reach)
    logm = np.where(mult > 0, np.log(np.maximum(mult, 1.0)), NEG_BIG)
    strips_b = np.stack([np.stack([-sl_b[2 * p] * delta + logm, -sl_b[2 * p + 1] * delta + logm])
                         for p in range(DIL_HEADS // 2)])
    return jnp.asarray(strips_a * LOG2E, F32), jnp.asarray(strips_b * LOG2E, F32)


def _hgrn_tables(C):
    levels = int(np.log2(C))
    t = np.arange(C)
    tri = np.stack([t[None, :] <= t[:, None], t[None, :] >= t[:, None]]).astype(np.float32)
    level = -np.ones((2, C, C), np.int32)
    for l in range(levels):
        m = 1 << l
        upper = (t % (2 * m)) >= m
        same = (t[:, None] // (2 * m)) == (t[None, :] // (2 * m))
        level[0][same & upper[:, None] & (~upper)[None, :]] = l
        level[1][same & (~upper)[:, None] & upper[None, :]] = l
    return tri, level


def _hgrn_kernel(q_ref, ff_ref, fb_ref, i_ref, og_ref, lbp_ref, ng_ref, tri_ref, level_ref,
                 o_ref, acc_ref, st_ref, *, layer):
    S = q_ref.shape[1]
    C = HGRN_C
    n_chunks = S // C
    levels = int(np.log2(C))
    row = lax.broadcasted_iota(jnp.int32, (C, LANES), 0)

    lbp = lbp_ref[...]
    e = jnp.exp(lbp - jnp.max(lbp, axis=0, keepdims=True))
    pr = e / jnp.sum(e, axis=0, keepdims=True)
    lb = jnp.zeros((1, LANES), F32)
    for j in range(1, layer + 1):
        lb = lb + pr[j:j + 1]

    st_ref[...] = jnp.zeros_like(st_ref)

    nt = (((1,), (1,)), ((), ()))

    def gates(it):
        x = (ff_ref if it["d"] == 0 else fb_ref)[0, it["rows"], :]
        f = lb + (1.0 - lb) * jax.nn.sigmoid(x)
        g = jnp.log(f)
        it["kk"] = 1.0 - f
        g0 = g.astype(BF16)
        r1 = g - g0.astype(F32)
        g1 = r1.astype(BF16)
        g2 = (r1 - g1.astype(F32)).astype(BF16)
        tri = tri_ref[it["d"]]
        it["run"] = (jnp.dot(tri, g0, preferred_element_type=F32)
                     + jnp.dot(tri, g1, preferred_element_type=F32)
                     + jnp.dot(tri, g2, preferred_element_type=F32))
        it["edge"] = it["run"]
        it["a"] = None

    def block_level(it, l):
        d, q, kk, run = it["d"], it["q"], it["kk"], it["run"]
        m = 1 << l
        zeros = jnp.zeros((m, LANES), F32)
        bnd = m - 1 if d == 0 else m
        qparts, kparts = [], []
        for s in range(0, C, 2 * m):
            lo, hi = slice(s, s + m), slice(s + m, s + 2 * m)
            mid = run[s + bnd:s + bnd + 1]
            ex_lo = jnp.exp2(jnp.abs(run[lo] - mid) * -LOG2E)
            ex_hi = jnp.exp2(jnp.abs(run[hi] - mid) * -LOG2E)
            if d == 0:
                qparts += [zeros, q[hi] * ex_hi]
                kparts += [kk[lo] * ex_lo, zeros]
            else:
                qparts += [q[lo] * ex_lo, zeros]
                kparts += [zeros, kk[hi] * ex_hi]
        r = lax.dot_general(jnp.concatenate(qparts, axis=0).astype(BF16),
                            jnp.concatenate(kparts, axis=0).astype(BF16), nt,
                            preferred_element_type=F32)
        it["a"] = r if it["a"] is None else jnp.where(level_ref[d] == l, r, it["a"])

    def sublane_level(it, l):
        d, edge = it["d"], it["edge"]
        m = 1 << l
        upper = (row & m) != 0
        dn = pltpu.roll(edge, m, 0)
        up = pltpu.roll(edge, C - m, 0)
        if d == 0:
            mid = jnp.where(upper, dn, edge)
            it["edge"] = jnp.where(upper, edge, up)
        else:
            mid = jnp.where(upper, edge, up)
            it["edge"] = jnp.where(upper, dn, edge)
        ex = jnp.exp2(jnp.abs(it["run"] - mid) * -LOG2E)
        r = lax.dot_general((it["q"] * ex).astype(BF16), (it["kk"] * ex).astype(BF16), nt,
                            preferred_element_type=F32)
        it["a"] = jnp.where(level_ref[d] == l, r, it["a"])

    def finish(it):
        d, q, kk, run, v = it["d"], it["q"], it["kk"], it["run"], it["v"]
        tot = run[C - 1:C] if d == 0 else run[0:1]
        st = st_ref[d]
        qe = (q * jnp.exp(run)).astype(BF16)
        o = lax.dot_general(qe, st.astype(BF16), nt, preferred_element_type=F32)
        o = o + jnp.dot(it["a"].astype(BF16), v.astype(BF16), preferred_element_type=F32)
        o = o + jnp.sum(q * kk, axis=-1, keepdims=True) * v
        kh = (kk * jnp.exp(tot - run)).astype(BF16)
        st_ref[d] = st * jnp.exp(tot) + jnp.dot(it["vt"], kh, preferred_element_type=F32)
        acc_ref[d, it["rows"], :] = o

    def body(c, carry):
        items = []
        for u in range(HGRN_GROUP):
            cf = c * HGRN_GROUP + u
            for d in range(2):
                cc = cf if d == 0 else n_chunks - 1 - cf
                rows = pl.ds(pl.multiple_of(cc * C, C), C)
                v = i_ref[0, rows, :]
                items.append(dict(d=d, rows=rows, q=q_ref[0, rows, :], v=v, vt=v.T.astype(BF16)))
        for it in items:
            gates(it)
        for l in range(levels - 1, SUBLANE_LEVELS - 1, -1):
            for it in items:
                block_level(it, l)
        for l in range(SUBLANE_LEVELS):
            for it in items:
                sublane_level(it, l)
        for it in items:
            finish(it)
        return carry

    lax.fori_loop(0, n_chunks // HGRN_GROUP, body, 0)
    og = og_ref[0]
    o = _rms(acc_ref[0] + acc_ref[1], ng_ref[...]) * (og * jax.nn.sigmoid(og))
    o_ref[0] = o.astype(o_ref.dtype)


def hgrn_mixer(zhg, lower_bounds, norm_g, layer):
    B, S, _ = zhg.shape
    tri, level = _hgrn_tables(HGRN_C)
    tri = jnp.asarray(tri, BF16)
    level = jnp.asarray(level)
    depth = lower_bounds.shape[0]
    col = lambda g: pl.BlockSpec((1, S, LANES), lambda b, h: (b, 0, g * HGRN_HEADS + h))
    return pl.pallas_call(
        functools.partial(_hgrn_kernel, layer=layer),
        grid=(B, HGRN_HEADS),
        in_specs=[col(0), col(1), col(2), col(3), col(4),
                  pl.BlockSpec((depth, LANES), lambda b, h: (0, h)),
                  _resident((1, LANES)), _resident(tri.shape), _resident(level.shape)],
        out_specs=pl.BlockSpec((1, S, LANES), lambda b, h: (b, 0, h)),
        out_shape=jax.ShapeDtypeStruct((B, S, HGRN_HEADS * LANES), BF16),
        scratch_shapes=[pltpu.VMEM((2, S, LANES), F32), pltpu.VMEM((2, LANES, LANES), F32)],
        compiler_params=_cparams(("parallel", "parallel")),
        name="hgrn",
    )(zhg, zhg, zhg, zhg, zhg, lower_bounds, norm_g.reshape(1, LANES), tri, level)


def _merge_kernel(h_ref, u_ref, ya_ref, yb_ref, yc_ref, wg_ref, wa_ref, wb_ref, wc_ref, wo_ref,
                  o_ref):
    D = h_ref.shape[1]
    u = u_ref[...]
    merged = None
    for j, (y_ref, w_ref) in enumerate(((ya_ref, wa_ref), (yb_ref, wb_ref), (yc_ref, wc_ref))):
        gate = jax.nn.sigmoid(jnp.dot(u, wg_ref[:, j * D:(j + 1) * D], preferred_element_type=F32))
        term = gate * jnp.dot(y_ref[...], w_ref[...], preferred_element_type=F32)
        merged = term if merged is None else merged + term
    o_ref[...] = h_ref[...] + jnp.dot(merged.astype(BF16), wo_ref[...], preferred_element_type=F32)


def merge_block(h, u, ya, yb, yc, wg, wa, wb, wc, wo):
    T, D = h.shape
    tm = min(FFN_TM, T)
    row = lambda w: pl.BlockSpec((tm, w), lambda i: (i, 0))
    return pl.pallas_call(
        _merge_kernel,
        grid=(T // tm,),
        in_specs=[row(D), row(D), row(ya.shape[1]), row(yb.shape[1]), row(yc.shape[1]),
                  _resident(wg.shape), _resident(wa.shape), _resident(wb.shape),
                  _resident(wc.shape), _resident(wo.shape)],
        out_specs=row(D),
        out_shape=jax.ShapeDtypeStruct((T, D), F32),
        compiler_params=_cparams(("parallel",)),
        name="merge",
    )(h, u, ya, yb, yc, wg, wa, wb, wc, wo)


def kernel(x, ffn1_norm, ffn1_w_gate, ffn1_w_up, ffn1_w_down, mix_norm, w_in, diff_lambda_q1, diff_lambda_k1, diff_lambda_q2, diff_lambda_k2, diff_subln, hgrn_lower_bounds, hgrn_norm, w_branch_a, w_branch_b, w_branch_c, w_out, ffn2_norm, ffn2_w_gate, ffn2_w_up, ffn2_w_down, final_norm):
    B, S, D = x.shape
    T = B * S
    depth = w_in.shape[0]
    bf = lambda w: w.astype(BF16)
    strips_a, strips_b = _bias_strips(S, min(ATT_TQ, S))
    col = np.arange(w_in.shape[2])
    is_q = (col < 512) | ((col >= 3 * 512) & (col < 4 * 512))
    col_scale = jnp.asarray(np.where(is_q, LOG2E * HEAD_DIM ** -0.5, 1.0), F32)
    h = x.reshape(T, D)
    for l in range(depth):
        h, u = ffn_block(h, ffn1_norm[l], bf(ffn1_w_gate[l]), bf(ffn1_w_up[l]), bf(ffn1_w_down[l]),
                         mix_norm[l], "hn", BF16)
        w = bf(w_in[l] * col_scale)
        zatt = project(u, w[:, :ATT_W], BF16, tn=ATT_W // 2).reshape(B, S, ATT_W)
        zhg = project(u, w[:, ATT_W:ATT_W + HG_W], F32, tn=HG_W // 2).reshape(B, S, HG_W)
        lam_init = 0.8 - 0.6 * float(np.exp(-0.3 * l))
        lamv = jnp.stack([diff_lambda_q1[l], diff_lambda_k1[l], diff_lambda_q2[l], diff_lambda_k2[l]])
        subln = diff_subln[l].reshape(1, 2 * HEAD_DIM)
        ya = attention_pairs(zatt, strips_a, lamv, subln, mode="diff", lam_init=lam_init,
                             q_blk=0, k_blk=4, v_blk=8)
        yb = attention_pairs(zatt, strips_b, lamv, subln, mode="dil", lam_init=lam_init,
                             q_blk=12, k_blk=16, v_blk=20)
        yc = hgrn_mixer(zhg, hgrn_lower_bounds, hgrn_norm[l], l)
        h = merge_block(h, u, ya.reshape(T, -1), yb.reshape(T, -1), yc.reshape(T, -1),
                        w[:, ATT_W + HG_W:], bf(w_branch_a[l]), bf(w_branch_b[l]),
                        bf(w_branch_c[l]), bf(w_out[l]))
        (h,) = ffn_block(h, ffn2_norm[l], bf(ffn2_w_gate[l]), bf(ffn2_w_up[l]), bf(ffn2_w_down[l]),
                         final_norm, "n" if l == depth - 1 else "h")
    return h.reshape(B, S, D)
```

```python
import functools
import itertools

import numpy as np
import jax
import jax.numpy as jnp
from jax import lax
from jax.experimental import pallas as pl
from jax.experimental.pallas import tpu as pltpu

F32 = jnp.float32
BF16 = jnp.bfloat16

D_MODEL = 1024
HEAD_DIM = 64
DIFF_HEADS = 4
DIL_HEADS = 8
DIL_PATTERNS = ((128, 1), (512, 4), (2048, 16))
HGRN_HEADS = 4
HGRN_DK = 128
N_BRANCH = 3
D_FF = 2816
NORM_EPS = 1e-6
N_ALIBI = DIFF_HEADS + DIL_HEADS
ALIBI_A_IDX = (0, 3, 6, 9)
ALIBI_B_IDX = (1, 2, 4, 5, 7, 8, 10, 11)
ATT_W = 6 * 512
HG_W = 5 * 512

LANES = 128
SUBLANE_LEVELS = 3
VMEM_LIMIT = 56 * 1024 * 1024

FFN_TM = 512
MM_TM = 1024
ATT_TQ = 512
ATT_TILES = 4
ATT_KC = 512
HGRN_C = 128
HGRN_GROUP = 8
NEG_BIG = -1e30
LOG2E = float(np.log2(np.e))


def _cparams(sem):
    return pltpu.CompilerParams(dimension_semantics=sem, vmem_limit_bytes=VMEM_LIMIT)


def _resident(shape):
    nd = len(shape)
    return pl.BlockSpec(shape, lambda *_: (0,) * nd, pipeline_mode=pl.Buffered(1))


def _rms(x, g):
    return x * lax.rsqrt(jnp.mean(x * x, axis=-1, keepdims=True) + NORM_EPS) * g


def _ffn_kernel(h_ref, g1_ref, wg_ref, wu_ref, wd_ref, g2_ref, *out_refs, emit):
    h = h_ref[...]
    n = _rms(h, g1_ref[...]).astype(BF16)
    a = jnp.dot(n, wg_ref[...], preferred_element_type=F32)
    b = jnp.dot(n, wu_ref[...], preferred_element_type=F32)
    t = (a * jax.nn.sigmoid(a) * b).astype(BF16)
    hout = h + 0.5 * jnp.dot(t, wd_ref[...], preferred_element_type=F32)
    if "h" in emit:
        out_refs[0][...] = hout
    if "n" in emit:
        out_refs[-1][...] = _rms(hout, g2_ref[...]).astype(out_refs[-1].dtype)


def ffn_block(h, g1, wg, wu, wd, g2, emit, n_dtype=F32):
    T, D = h.shape
    tm = min(FFN_TM, T)
    row = pl.BlockSpec((tm, D), lambda i: (i, 0))
    out_dtypes = [F32] * ("h" in emit) + [n_dtype] * ("n" in emit)
    return pl.pallas_call(
        functools.partial(_ffn_kernel, emit=emit),
        grid=(T // tm,),
        in_specs=[row, _resident((1, D)), _resident(wg.shape), _resident(wu.shape),
                  _resident(wd.shape), _resident((1, D))],
        out_specs=[row] * len(out_dtypes),
        out_shape=[jax.ShapeDtypeStruct((T, D), dt) for dt in out_dtypes],
        compiler_params=_cparams(("parallel",)),
        name="ffn",
    )(h, g1.reshape(1, D), wg, wu, wd, g2.reshape(1, D))


def _proj_kernel(x_ref, w_ref, o_ref):
    o_ref[...] = jnp.dot(x_ref[...], w_ref[...], preferred_element_type=F32).astype(o_ref.dtype)


def project(x, w, out_dtype, tn):
    T, K = x.shape
    N = w.shape[1]
    tm = min(MM_TM, T)
    return pl.pallas_call(
        _proj_kernel,
        grid=(T // tm, N // tn),
        in_specs=[pl.BlockSpec((tm, K), lambda i, j: (i, 0)),
                  pl.BlockSpec((K, tn), lambda i, j: (0, j))],
        out_specs=pl.BlockSpec((tm, tn), lambda i, j: (i, j)),
        out_shape=jax.ShapeDtypeStruct((T, N), out_dtype),
        compiler_params=_cparams(("parallel", "arbitrary")),
        name="proj_in",
    )(x, w)


def _attn_kernel(q_ref, k_ref, v_ref, strip_ref, lamv_ref, subln_ref, o_ref, vx_ref, s_ref, *,
                 mode, lam_init, reach):
    tq = strip_ref.shape[2]
    n_tiles = q_ref.shape[1] // tq
    S = k_ref.shape[1]
    qi = pl.program_id(2)

    @pl.when(qi == 0)
    def _():
        vx_ref[:, :LANES] = v_ref[0]
        vx_ref[:, LANES:] = jnp.ones((S, LANES), BF16)

    lane = lax.broadcasted_iota(jnp.int32, (tq, LANES), 1)
    first = lane < HEAD_DIM
    n_strip = strip_ref.shape[1]
    kc = ATT_KC
    n_kc = S // kc
    if mode == "diff":
        lv = lamv_ref[...]
        lam = (jnp.exp(jnp.sum(lv[0:1] * lv[1:2], axis=-1, keepdims=True))
               - jnp.exp(jnp.sum(lv[2:3] * lv[3:4], axis=-1, keepdims=True)) + lam_init)

    def stacked_q(t):
        q = q_ref[0, t * tq:(t + 1) * tq, :]
        zero = jnp.zeros_like(q)
        return jnp.concatenate([jnp.where(first, q, zero), jnp.where(first, zero, q)], axis=0)

    def score_chunk(t, q2, c, m):
        sc = lax.dot_general(q2, k_ref[0, c * kc:(c + 1) * kc, :], (((1,), (1,)), ((), ())),
                             preferred_element_type=F32)
        off = pl.multiple_of(S - (qi * n_tiles + t) * tq + c * kc, LANES)
        bias = [strip_ref[0, e % n_strip, :, pl.ds(off, kc)] for e in range(n_strip)]
        sc = sc + jnp.concatenate([bias[0], bias[-1]], axis=0)
        s_ref[t % 2, :, c * kc:(c + 1) * kc] = sc
        mc = jnp.max(sc, axis=-1, keepdims=True)
        return mc if m is None else jnp.maximum(m, mc)

    def value_chunk(t, c, m, ox):
        p = jnp.exp2(s_ref[t % 2, :, c * kc:(c + 1) * kc] - m).astype(BF16)
        for e in range(2):
            part = jnp.dot(p[e * tq:(e + 1) * tq], vx_ref[c * kc:(c + 1) * kc, :],
                           preferred_element_type=F32)
            ox[e] = part if ox[e] is None else ox[e] + part

    def finish(t, ox):
        outs = [ox[e][:, :LANES] / ox[e][:, LANES:] for e in range(2)]
        if mode == "diff":
            o = outs[0] - lam * outs[1]
            o = _rms(o, subln_ref[...]) * (1.0 - lam_init)
        else:
            o = jnp.where(first, outs[0], outs[1])
        o_ref[0, t * tq:(t + 1) * tq, :] = o.astype(o_ref.dtype)

    def chunks(t):
        if reach is None or n_tiles * tq != S:
            return list(range(n_kc))
        lo, hi = t * tq - reach, (t + 1) * tq - 1 + reach
        return [c for c in range(n_kc) if (c + 1) * kc - 1 >= lo and c * kc <= hi]

    q2 = stacked_q(0)
    m = None
    for c in chunks(0):
        m = score_chunk(0, q2, c, m)
    for t in range(n_tiles):
        ox = [None, None]
        nxt = chunks(t + 1) if t + 1 < n_tiles else []
        if nxt:
            q2_next, m_next = stacked_q(t + 1), None
        for c, cn in itertools.zip_longest(chunks(t), nxt):
            if c is not None:
                value_chunk(t, c, m, ox)
            if cn is not None:
                m_next = score_chunk(t + 1, q2_next, cn, m_next)
        finish(t, ox)
        if nxt:
            m = m_next


def attention_pairs(zatt, strips, lamv, subln, *, mode, lam_init, q_blk, k_blk, v_blk):
    B, S, _ = zatt.shape
    n_pairs = strips.shape[0]
    tq = strips.shape[2]
    n_tiles = min(ATT_TILES, S // tq)
    rows = n_tiles * tq
    reach = max((w // (2 * d)) * d for w, d in DIL_PATTERNS) if mode == "dil" else None
    return pl.pallas_call(
        functools.partial(_attn_kernel, mode=mode, lam_init=lam_init, reach=reach),
        grid=(n_pairs, B, S // rows),
        in_specs=[
            pl.BlockSpec((1, rows, LANES), lambda p, b, i: (b, i, q_blk + p)),
            pl.BlockSpec((1, S, LANES), lambda p, b, i: (b, 0, k_blk + p)),
            pl.BlockSpec((1, S, LANES), lambda p, b, i: (b, 0, v_blk + p)),
            pl.BlockSpec((1, strips.shape[1], tq, 2 * S), lambda p, b, i: (p, 0, 0, 0),
                         pipeline_mode=pl.Buffered(1)),
            _resident(lamv.shape),
            _resident(subln.shape),
        ],
        out_specs=pl.BlockSpec((1, rows, LANES), lambda p, b, i: (b, i, p)),
        out_shape=jax.ShapeDtypeStruct((B, S, n_pairs * LANES), BF16),
        scratch_shapes=[pltpu.VMEM((S, 2 * LANES), BF16), pltpu.VMEM((2, 2 * tq, S), F32)],
        compiler_params=_cparams(("parallel", "parallel", "arbitrary")),
        name="attn_" + mode,
    )(zatt, zatt, zatt, strips, lamv, subln)


def _alibi_slopes():
    h = np.arange(1, N_ALIBI + 1, dtype=np.float64)
    s = np.exp2(-8.0 * h / N_ALIBI)
    return s[list(ALIBI_A_IDX)], s[list(ALIBI_B_IDX)]


def _bias_strips(S, tq):
    sl_a, sl_b = _alibi_slopes()
    r = np.arange(tq, dtype=np.float64)[:, None]
    c = np.arange(2 * S, dtype=np.float64)[None, :]
    delta = np.abs(r - (c - S))
    strips_a = np.stack([(-s * delta)[None] for s in sl_a])
    mult = np.zeros_like(delta)
    for window, dilation in DIL_PATTERNS:
        reach = (window // (2 * dilation)) * dilation
        mult += ((delta % dilation) == 0) & (delta <= reach)
    logm = np.where(mult > 0, np.log(np.maximum(mult, 1.0)), NEG_BIG)
    strips_b = np.stack([np.stack([-sl_b[2 * p] * delta + logm, -sl_b[2 * p + 1] * delta + logm])
                         for p in range(DIL_HEADS // 2)])
    return jnp.asarray(strips_a * LOG2E, F32), jnp.asarray(strips_b * LOG2E, F32)


def _hgrn_tables(C):
    levels = int(np.log2(C))
    t = np.arange(C)
    tri = np.stack([t[None, :] <= t[:, None], t[None, :] >= t[:, None]]).astype(np.float32)
    level = -np.ones((2, C, C), np.int32)
    for l in range(levels):
        m = 1 << l
        upper = (t % (2 * m)) >= m
        same = (t[:, None] // (2 * m)) == (t[None, :] // (2 * m))
        level[0][same & upper[:, None] & (~upper)[None, :]] = l
        level[1][same & (~upper)[:, None] & upper[None, :]] = l
    return tri, level


def _hgrn_kernel(q_ref, ff_ref, fb_ref, i_ref, og_ref, lbp_ref, ng_ref, tri_ref, level_ref,
                 o_ref, acc_ref, st_ref, *, layer):
    S = q_ref.shape[1]
    C = HGRN_C
    n_chunks = S // C
    levels = int(np.log2(C))
    row = lax.broadcasted_iota(jnp.int32, (C, LANES), 0)

    lbp = lbp_ref[...]
    e = jnp.exp(lbp - jnp.max(lbp, axis=0, keepdims=True))
    pr = e / jnp.sum(e, axis=0, keepdims=True)
    lb = jnp.zeros((1, LANES), F32)
    for j in range(1, layer + 1):
        lb = lb + pr[j:j + 1]

    st_ref[...] = jnp.zeros_like(st_ref)

    nt = (((1,), (1,)), ((), ()))

    def gates(it):
        x = (ff_ref if it["d"] == 0 else fb_ref)[0, it["rows"], :]
        f = lb + (1.0 - lb) * jax.nn.sigmoid(x)
        g = jnp.log(f)
        it["kk"] = 1.0 - f
        g0 = g.astype(BF16)
        r1 = g - g0.astype(F32)
        g1 = r1.astype(BF16)
        g2 = (r1 - g1.astype(F32)).astype(BF16)
        tri = tri_ref[it["d"]]
        it["run"] = (jnp.dot(tri, g0, preferred_element_type=F32)
                     + jnp.dot(tri, g1, preferred_element_type=F32)
                     + jnp.dot(tri, g2, preferred_element_type=F32))
        it["edge"] = it["run"]
        it["a"] = None

    def block_level(it, l):
        d, q, kk, run = it["d"], it["q"], it["kk"], it["run"]
        m = 1 << l
        zeros = jnp.zeros((m, LANES), F32)
        bnd = m - 1 if d == 0 else m
        qparts, kparts = [], []
        for s in range(0, C, 2 * m):
            lo, hi = slice(s, s + m), slice(s + m, s + 2 * m)
            mid = run[s + bnd:s + bnd + 1]
            ex_lo = jnp.exp2(jnp.abs(run[lo] - mid) * -LOG2E)
            ex_hi = jnp.exp2(jnp.abs(run[hi] - mid) * -LOG2E)
            if d == 0:
                qparts += [zeros, q[hi] * ex_hi]
                kparts += [kk[lo] * ex_lo, zeros]
            else:
                qparts += [q[lo] * ex_lo, zeros]
                kparts += [zeros, kk[hi] * ex_hi]
        r = lax.dot_general(jnp.concatenate(qparts, axis=0).astype(BF16),
                            jnp.concatenate(kparts, axis=0).astype(BF16), nt,
                            preferred_element_type=F32)
        it["a"] = r if it["a"] is None else jnp.where(level_ref[d] == l, r, it["a"])

    def sublane_level(it, l):
        d, edge = it["d"], it["edge"]
        m = 1 << l
        upper = (row & m) != 0
        dn = pltpu.roll(edge, m, 0)
        up = pltpu.roll(edge, C - m, 0)
        if d == 0:
            mid = jnp.where(upper, dn, edge)
            it["edge"] = jnp.where(upper, edge, up)
        else:
            mid = jnp.where(upper, edge, up)
            it["edge"] = jnp.where(upper, dn, edge)
        ex = jnp.exp2(jnp.abs(it["run"] - mid) * -LOG2E)
        r = lax.dot_general((it["q"] * ex).astype(BF16), (it["kk"] * ex).astype(BF16), nt,
                            preferred_element_type=F32)
        it["a"] = jnp.where(level_ref[d] == l, r, it["a"])

    def finish(it):
        d, q, kk, run, v = it["d"], it["q"], it["kk"], it["run"], it["v"]
        tot = run[C - 1:C] if d == 0 else run[0:1]
        st = st_ref[d]
        qe = (q * jnp.exp(run)).astype(BF16)
        o = lax.dot_general(qe, st.astype(BF16), nt, preferred_element_type=F32)
        o = o + jnp.dot(it["a"].astype(BF16), v.astype(BF16), preferred_element_type=F32)
        o = o + jnp.sum(q * kk, axis=-1, keepdims=True) * v
        kh = (kk * jnp.exp(tot - run)).astype(BF16)
        st_ref[d] = st * jnp.exp(tot) + jnp.dot(it["vt"], kh, preferred_element_type=F32)
        acc_ref[d, it["rows"], :] = o

    def body(c, carry):
        items = []
        for u in range(HGRN_GROUP):
            cf = c * HGRN_GROUP + u
            for d in range(2):
                cc = cf if d == 0 else n_chunks - 1 - cf
                rows = pl.ds(pl.multiple_of(cc * C, C), C)
                v = i_ref[0, rows, :]
                items.append(dict(d=d, rows=rows, q=q_ref[0, rows, :], v=v, vt=v.T.astype(BF16)))
        for it in items:
            gates(it)
        for l in range(levels - 1, SUBLANE_LEVELS - 1, -1):
            for it in items:
                block_level(it, l)
        for l in range(SUBLANE_LEVELS):
            for it in items:
                sublane_level(it, l)
        for it in items:
            finish(it)
        return carry

    lax.fori_loop(0, n_chunks // HGRN_GROUP, body, 0)
    og = og_ref[0]
    o = _rms(acc_ref[0] + acc_ref[1], ng_ref[...]) * (og * jax.nn.sigmoid(og))
    o_ref[0] = o.astype(o_ref.dtype)


def hgrn_mixer(zhg, lower_bounds, norm_g, layer):
    B, S, _ = zhg.shape
    tri, level = _hgrn_tables(HGRN_C)
    tri = jnp.asarray(tri, BF16)
    level = jnp.asarray(level)
    depth = lower_bounds.shape[0]
    col = lambda g: pl.BlockSpec((1, S, LANES), lambda b, h: (b, 0, g * HGRN_HEADS + h))
    return pl.pallas_call(
        functools.partial(_hgrn_kernel, layer=layer),
        grid=(B, HGRN_HEADS),
        in_specs=[col(0), col(1), col(2), col(3), col(4),
                  pl.BlockSpec((depth, LANES), lambda b, h: (0, h)),
                  _resident((1, LANES)), _resident(tri.shape), _resident(level.shape)],
        out_specs=pl.BlockSpec((1, S, LANES), lambda b, h: (b, 0, h)),
        out_shape=jax.ShapeDtypeStruct((B, S, HGRN_HEADS * LANES), BF16),
        scratch_shapes=[pltpu.VMEM((2, S, LANES), F32), pltpu.VMEM((2, LANES, LANES), F32)],
        compiler_params=_cparams(("parallel", "parallel")),
        name="hgrn",
    )(zhg, zhg, zhg, zhg, zhg, lower_bounds, norm_g.reshape(1, LANES), tri, level)


def _merge_kernel(h_ref, u_ref, ya_ref, yb_ref, yc_ref, wg_ref, wa_ref, wb_ref, wc_ref, wo_ref,
                  o_ref):
    D = h_ref.shape[1]
    u = u_ref[...]
    merged = None
    for j, (y_ref, w_ref) in enumerate(((ya_ref, wa_ref), (yb_ref, wb_ref), (yc_ref, wc_ref))):
        gate = jax.nn.sigmoid(jnp.dot(u, wg_ref[:, j * D:(j + 1) * D], preferred_element_type=F32))
        term = gate * jnp.dot(y_ref[...], w_ref[...], preferred_element_type=F32)
        merged = term if merged is None else merged + term
    o_ref[...] = h_ref[...] + jnp.dot(merged.astype(BF16), wo_ref[...], preferred_element_type=F32)


def merge_block(h, u, ya, yb, yc, wg, wa, wb, wc, wo):
    T, D = h.shape
    tm = min(FFN_TM, T)
    row = lambda w: pl.BlockSpec((tm, w), lambda i: (i, 0))
    return pl.pallas_call(
        _merge_kernel,
        grid=(T // tm,),
        in_specs=[row(D), row(D), row(ya.shape[1]), row(yb.shape[1]), row(yc.shape[1]),
                  _resident(wg.shape), _resident(wa.shape), _resident(wb.shape),
                  _resident(wc.shape), _resident(wo.shape)],
        out_specs=row(D),
        out_shape=jax.ShapeDtypeStruct((T, D), F32),
        compiler_params=_cparams(("parallel",)),
        name="merge",
    )(h, u, ya, yb, yc, wg, wa, wb, wc, wo)


def kernel(x, ffn1_norm, ffn1_w_gate, ffn1_w_up, ffn1_w_down, mix_norm, w_in, diff_lambda_q1, diff_lambda_k1, diff_lambda_q2, diff_lambda_k2, diff_subln, hgrn_lower_bounds, hgrn_norm, w_branch_a, w_branch_b, w_branch_c, w_out, ffn2_norm, ffn2_w_gate, ffn2_w_up, ffn2_w_down, final_norm):
    B, S, D = x.shape
    T = B * S
    depth = w_in.shape[0]
    bf = lambda w: w.astype(BF16)
    strips_a, strips_b = _bias_strips(S, min(ATT_TQ, S))
    col = np.arange(w_in.shape[2])
    is_q = (col < 512) | ((col >= 3 * 512) & (col < 4 * 512))
    col_scale = jnp.asarray(np.where(is_q, LOG2E * HEAD_DIM ** -0.5, 1.0), F32)
    h = x.reshape(T, D)
    for l in range(depth):
        h, u = ffn_block(h, ffn1_norm[l], bf(ffn1_w_gate[l]), bf(ffn1_w_up[l]), bf(ffn1_w_down[l]),
                         mix_norm[l], "hn", BF16)
        w = bf(w_in[l] * col_scale)
        zatt = project(u, w[:, :ATT_W], BF16, tn=ATT_W // 2).reshape(B, S, ATT_W)
        zhg = project(u, w[:, ATT_W:ATT_W + HG_W], F32, tn=HG_W // 2).reshape(B, S, HG_W)
        lam_init = 0.8 - 0.6 * float(np.exp(-0.3 * l))
        lamv = jnp.stack([diff_lambda_q1[l], diff_lambda_k1[l], diff_lambda_q2[l], diff_lambda_k2[l]])
        subln = diff_subln[l].reshape(1, 2 * HEAD_DIM)
        ya = attention_pairs(zatt, strips_a, lamv, subln, mode="diff", lam_init=lam_init,
                             q_blk=0, k_blk=4, v_blk=8)
        yb = attention_pairs(zatt, strips_b, lamv, subln, mode="dil", lam_init=lam_init,
                             q_blk=12, k_blk=16, v_blk=20)
        yc = hgrn_mixer(zhg, hgrn_lower_bounds, hgrn_norm[l], l)
        h = merge_block(h, u, ya.reshape(T, -1), yb.reshape(T, -1), yc.reshape(T, -1),
                        w[:, ATT_W + HG_W:], bf(w_branch_a[l]), bf(w_branch_b[l]),
                        bf(w_branch_c[l]), bf(w_out[l]))
        (h,) = ffn_block(h, ffn2_norm[l], bf(ffn2_w_gate[l]), bf(ffn2_w_up[l]), bf(ffn2_w_down[l]),
                         final_norm, "n" if l == depth - 1 else "h")
    return h.reshape(B, S, D)
```

```python
import functools
import itertools

import numpy as np
import jax
import jax.numpy as jnp
from jax import lax
from jax.experimental import pallas as pl
from jax.experimental.pallas import tpu as pltpu

F32 = jnp.float32
BF16 = jnp.bfloat16

D_MODEL = 1024
HEAD_DIM = 64
DIFF_HEADS = 4
DIL_HEADS = 8
DIL_PATTERNS = ((128, 1), (512, 4), (2048, 16))
HGRN_HEADS = 4
HGRN_DK = 128
N_BRANCH = 3
D_FF = 2816
NORM_EPS = 1e-6
N_ALIBI = DIFF_HEADS + DIL_HEADS
ALIBI_A_IDX = (0, 3, 6, 9)
ALIBI_B_IDX = (1, 2, 4, 5, 7, 8, 10, 11)
ATT_W = 6 * 512
HG_W = 5 * 512

LANES = 128
SUBLANE_LEVELS = 3
BF16_ROWS = 16
VMEM_LIMIT = 56 * 1024 * 1024

FFN_TM = 512
FFN_SUB = 2
MM_TM = 1024
ATT_TQ = 512
ATT_TILES = 4
ATT_KC = 512
HGRN_C = 128
HGRN_GROUP = 8
NEG_BIG = -1e30
LOG2E = float(np.log2(np.e))


def _cparams(sem):
    return pltpu.CompilerParams(dimension_semantics=sem, vmem_limit_bytes=VMEM_LIMIT)


def _resident(shape):
    nd = len(shape)
    return pl.BlockSpec(shape, lambda *_: (0,) * nd, pipeline_mode=pl.Buffered(1))


def _rms(x, g):
    return x * lax.rsqrt(jnp.mean(x * x, axis=-1, keepdims=True) + NORM_EPS) * g


def _ffn_kernel(h_ref, g1_ref, wg_ref, wu_ref, wd_ref, g2_ref, *out_refs, emit):
    ts = h_ref.shape[0] // FFN_SUB
    rows = [slice(s * ts, (s + 1) * ts) for s in range(FFN_SUB)]
    h = [h_ref[r, :] for r in rows]
    n = [_rms(x, g1_ref[...]).astype(BF16) for x in h]
    a = [jnp.dot(x, wg_ref[...], preferred_element_type=F32) for x in n]
    b = [jnp.dot(x, wu_ref[...], preferred_element_type=F32) for x in n]
    t = [(x * jax.nn.sigmoid(x) * y).astype(BF16) for x, y in zip(a, b)]
    hout = [x + 0.5 * jnp.dot(y, wd_ref[...], preferred_element_type=F32) for x, y in zip(h, t)]
    for r, x in zip(rows, hout):
        if "h" in emit:
            out_refs[0][r, :] = x
        if "n" in emit:
            out_refs[-1][r, :] = _rms(x, g2_ref[...]).astype(out_refs[-1].dtype)


def ffn_block(h, g1, wg, wu, wd, g2, emit, n_dtype=F32):
    T, D = h.shape
    tm = min(FFN_TM, T)
    row = pl.BlockSpec((tm, D), lambda i: (i, 0))
    out_dtypes = [F32] * ("h" in emit) + [n_dtype] * ("n" in emit)
    return pl.pallas_call(
        functools.partial(_ffn_kernel, emit=emit),
        grid=(T // tm,),
        in_specs=[row, _resident((1, D)), _resident(wg.shape), _resident(wu.shape),
                  _resident(wd.shape), _resident((1, D))],
        out_specs=[row] * len(out_dtypes),
        out_shape=[jax.ShapeDtypeStruct((T, D), dt) for dt in out_dtypes],
        compiler_params=_cparams(("parallel",)),
        name="ffn",
    )(h, g1.reshape(1, D), wg, wu, wd, g2.reshape(1, D))


def _proj_kernel(x_ref, w_ref, o_ref):
    o_ref[...] = jnp.dot(x_ref[...], w_ref[...], preferred_element_type=F32).astype(o_ref.dtype)


def project(x, w, out_dtype, tn):
    T, K = x.shape
    N = w.shape[1]
    tm = min(MM_TM, T)
    return pl.pallas_call(
        _proj_kernel,
        grid=(T // tm, N // tn),
        in_specs=[pl.BlockSpec((tm, K), lambda i, j: (i, 0)),
                  pl.BlockSpec((K, tn), lambda i, j: (0, j))],
        out_specs=pl.BlockSpec((tm, tn), lambda i, j: (i, j)),
        out_shape=jax.ShapeDtypeStruct((T, N), out_dtype),
        compiler_params=_cparams(("parallel", "arbitrary")),
        name="proj_in",
    )(x, w)


def _attn_kernel(q_ref, k_ref, v_ref, strip_ref, lamv_ref, subln_ref, o_ref, vx_ref, s_ref, *,
                 mode, lam_init, reach):
    tq = strip_ref.shape[2]
    n_tiles = q_ref.shape[1] // tq
    S = k_ref.shape[1]
    qi = pl.program_id(2)

    @pl.when(qi == 0)
    def _():
        vx_ref[:, :LANES] = v_ref[0]
        vx_ref[:, LANES:] = jnp.ones((S, LANES), BF16)

    lane = lax.broadcasted_iota(jnp.int32, (tq, LANES), 1)
    first = lane < HEAD_DIM
    n_strip = strip_ref.shape[1]
    kc = ATT_KC
    n_kc = S // kc
    if mode == "diff":
        lv = lamv_ref[...]
        lam = (jnp.exp(jnp.sum(lv[0:1] * lv[1:2], axis=-1, keepdims=True))
               - jnp.exp(jnp.sum(lv[2:3] * lv[3:4], axis=-1, keepdims=True)) + lam_init)

    def stacked_q(t):
        q = q_ref[0, t * tq:(t + 1) * tq, :]
        zero = jnp.zeros_like(q)
        return jnp.concatenate([jnp.where(first, q, zero), jnp.where(first, zero, q)], axis=0)

    def score_chunk(t, q2, c, m):
        sc = lax.dot_general(q2, k_ref[0, c * kc:(c + 1) * kc, :], (((1,), (1,)), ((), ())),
                             preferred_element_type=F32)
        off = pl.multiple_of(S - (qi * n_tiles + t) * tq + c * kc, LANES)
        bias = [strip_ref[0, e % n_strip, :, pl.ds(off, kc)] for e in range(n_strip)]
        sc = sc + jnp.concatenate([bias[0], bias[-1]], axis=0)
        s_ref[t % 2, :, c * kc:(c + 1) * kc] = sc
        mc = jnp.max(sc, axis=-1, keepdims=True)
        return mc if m is None else jnp.maximum(m, mc)

    def value_chunk(t, c, m, ox):
        p = jnp.exp2(s_ref[t % 2, :, c * kc:(c + 1) * kc] - m).astype(BF16)
        for e in range(2):
            part = jnp.dot(p[e * tq:(e + 1) * tq], vx_ref[c * kc:(c + 1) * kc, :],
                           preferred_element_type=F32)
            ox[e] = part if ox[e] is None else ox[e] + part

    def finish(t, ox):
        outs = [ox[e][:, :LANES] / ox[e][:, LANES:] for e in range(2)]
        if mode == "diff":
            o = outs[0] - lam * outs[1]
            o = _rms(o, subln_ref[...]) * (1.0 - lam_init)
        else:
            o = jnp.where(first, outs[0], outs[1])
        o_ref[0, t * tq:(t + 1) * tq, :] = o.astype(o_ref.dtype)

    def chunks(t):
        if reach is None or n_tiles * tq != S:
            return list(range(n_kc))
        lo, hi = t * tq - reach, (t + 1) * tq - 1 + reach
        return [c for c in range(n_kc) if (c + 1) * kc - 1 >= lo and c * kc <= hi]

    q2 = stacked_q(0)
    m = None
    for c in chunks(0):
        m = score_chunk(0, q2, c, m)
    for t in range(n_tiles):
        ox = [None, None]
        nxt = chunks(t + 1) if t + 1 < n_tiles else []
        if nxt:
            q2_next, m_next = stacked_q(t + 1), None
        for c, cn in itertools.zip_longest(chunks(t), nxt):
            if c is not None:
                value_chunk(t, c, m, ox)
            if cn is not None:
                m_next = score_chunk(t + 1, q2_next, cn, m_next)
        finish(t, ox)
        if nxt:
            m = m_next


def attention_pairs(zatt, strips, lamv, subln, *, mode, lam_init, q_blk, k_blk, v_blk):
    B, S, _ = zatt.shape
    n_pairs = strips.shape[0]
    tq = strips.shape[2]
    n_tiles = min(ATT_TILES, S // tq)
    rows = n_tiles * tq
    reach = max((w // (2 * d)) * d for w, d in DIL_PATTERNS) if mode == "dil" else None
    return pl.pallas_call(
        functools.partial(_attn_kernel, mode=mode, lam_init=lam_init, reach=reach),
        grid=(n_pairs, B, S // rows),
        in_specs=[
            pl.BlockSpec((1, rows, LANES), lambda p, b, i: (b, i, q_blk + p)),
            pl.BlockSpec((1, S, LANES), lambda p, b, i: (b, 0, k_blk + p)),
            pl.BlockSpec((1, S, LANES), lambda p, b, i: (b, 0, v_blk + p)),
            pl.BlockSpec((1, strips.shape[1], tq, 2 * S), lambda p, b, i: (p, 0, 0, 0),
                         pipeline_mode=pl.Buffered(1)),
            _resident(lamv.shape),
            _resident(subln.shape),
        ],
        out_specs=pl.BlockSpec((1, rows, LANES), lambda p, b, i: (b, i, p)),
        out_shape=jax.ShapeDtypeStruct((B, S, n_pairs * LANES), BF16),
        scratch_shapes=[pltpu.VMEM((S, 2 * LANES), BF16), pltpu.VMEM((2, 2 * tq, S), F32)],
        compiler_params=_cparams(("parallel", "parallel", "arbitrary")),
        name="attn_" + mode,
    )(zatt, zatt, zatt, strips, lamv, subln)


def _alibi_slopes():
    h = np.arange(1, N_ALIBI + 1, dtype=np.float64)
    s = np.exp2(-8.0 * h / N_ALIBI)
    return s[list(ALIBI_A_IDX)], s[list(ALIBI_B_IDX)]


def _bias_strips(S, tq):
    sl_a, sl_b = _alibi_slopes()
    r = np.arange(tq, dtype=np.float64)[:, None]
    c = np.arange(2 * S, dtype=np.float64)[None, :]
    delta = np.abs(r - (c - S))
    strips_a = np.stack([(-s * delta)[None] for s in sl_a])
    mult = np.zeros_like(delta)
    for window, dilation in DIL_PATTERNS:
        reach = (window // (2 * dilation)) * dilation
        mult += ((delta % dilation) == 0) & (delta <= reach)
    logm = np.where(mult > 0, np.log(np.maximum(mult, 1.0)), NEG_BIG)
    strips_b = np.stack([np.stack([-sl_b[2 * p] * delta + logm, -sl_b[2 * p + 1] * delta + logm])
                         for p in range(DIL_HEADS // 2)])
    return jnp.asarray(strips_a * LOG2E, F32), jnp.asarray(strips_b * LOG2E, F32)


def _hgrn_tables(C):
    levels = int(np.log2(C))
    t = np.arange(C)
    tri = np.stack([t[None, :] <= t[:, None], t[None, :] >= t[:, None]]).astype(np.float32)
    level = -np.ones((2, C, C), np.int32)
    for l in range(levels):
        m = 1 << l
        upper = (t % (2 * m)) >= m
        same = (t[:, None] // (2 * m)) == (t[None, :] // (2 * m))
        level[0][same & upper[:, None] & (~upper)[None, :]] = l
        level[1][same & (~upper)[:, None] & upper[None, :]] = l
    return tri, level


def _hgrn_kernel(q_ref, ff_ref, fb_ref, i_ref, og_ref, lbp_ref, ng_ref, tri_ref, level_ref,
                 o_ref, acc_ref, st_ref, *, layer):
    S = q_ref.shape[1]
    C = HGRN_C
    n_chunks = S // C
    levels = int(np.log2(C))
    row = lax.broadcasted_iota(jnp.int32, (C, LANES), 0)

    lbp = lbp_ref[...]
    e = jnp.exp(lbp - jnp.max(lbp, axis=0, keepdims=True))
    pr = e / jnp.sum(e, axis=0, keepdims=True)
    lb = jnp.zeros((1, LANES), F32)
    for j in range(1, layer + 1):
        lb = lb + pr[j:j + 1]

    st_ref[...] = jnp.zeros_like(st_ref)

    nt = (((1,), (1,)), ((), ()))

    def gates(it):
        x = (ff_ref if it["d"] == 0 else fb_ref)[0, it["rows"], :]
        f = lb + (1.0 - lb) * jax.nn.sigmoid(x)
        g = jnp.log(f)
        it["kk"] = 1.0 - f
        it["qb"], it["kb"] = it["q"].astype(BF16), it["kk"].astype(BF16)
        g0 = g.astype(BF16)
        r1 = g - g0.astype(F32)
        g1 = r1.astype(BF16)
        g2 = (r1 - g1.astype(F32)).astype(BF16)
        tri = tri_ref[it["d"]]
        it["run"] = (jnp.dot(tri, g0, preferred_element_type=F32)
                     + jnp.dot(tri, g1, preferred_element_type=F32)
                     + jnp.dot(tri, g2, preferred_element_type=F32))
        it["edge"] = it["run"]
        it["a"] = None

    def block_level(it, l):
        d, run = it["d"], it["run"]
        m = 1 << l
        dt = BF16 if m % BF16_ROWS == 0 else F32
        q, kk = (it["qb"], it["kb"]) if dt == BF16 else (it["q"], it["kk"])
        zeros = jnp.zeros((m, LANES), dt)
        bnd = m - 1 if d == 0 else m
        qparts, kparts = [], []
        for s in range(0, C, 2 * m):
            lo, hi = slice(s, s + m), slice(s + m, s + 2 * m)
            mid = run[s + bnd:s + bnd + 1]
            ex_lo = jnp.exp2(jnp.abs(run[lo] - mid) * -LOG2E).astype(dt)
            ex_hi = jnp.exp2(jnp.abs(run[hi] - mid) * -LOG2E).astype(dt)
            if d == 0:
                qparts += [zeros, q[hi] * ex_hi]
                kparts += [kk[lo] * ex_lo, zeros]
            else:
                qparts += [q[lo] * ex_lo, zeros]
                kparts += [zeros, kk[hi] * ex_hi]
        r = lax.dot_general(jnp.concatenate(qparts, axis=0).astype(BF16),
                            jnp.concatenate(kparts, axis=0).astype(BF16), nt,
                            preferred_element_type=F32)
        it["a"] = r if it["a"] is None else jnp.where(level_ref[d] == l, r, it["a"])

    def sublane_level(it, l):
        d, edge = it["d"], it["edge"]
        m = 1 << l
        upper = (row & m) != 0
        dn = pltpu.roll(edge, m, 0)
        up = pltpu.roll(edge, C - m, 0)
        if d == 0:
            mid = jnp.where(upper, dn, edge)
            it["edge"] = jnp.where(upper, edge, up)
        else:
            mid = jnp.where(upper, edge, up)
            it["edge"] = jnp.where(upper, dn, edge)
        ex = jnp.exp2(jnp.abs(it["run"] - mid) * -LOG2E).astype(BF16)
        r = lax.dot_general(it["qb"] * ex, it["kb"] * ex, nt, preferred_element_type=F32)
        it["a"] = jnp.where(level_ref[d] == l, r, it["a"])

    def finish(it):
        d, q, kk, run, v = it["d"], it["q"], it["kk"], it["run"], it["v"]
        tot = run[C - 1:C] if d == 0 else run[0:1]
        st = st_ref[d]
        qe = it["qb"] * jnp.exp(run).astype(BF16)
        o = lax.dot_general(qe, st.astype(BF16), nt, preferred_element_type=F32)
        o = o + jnp.dot(it["a"].astype(BF16), v.astype(BF16), preferred_element_type=F32)
        o = o + jnp.sum(q * kk, axis=-1, keepdims=True) * v
        kh = it["kb"] * jnp.exp(tot - run).astype(BF16)
        st_ref[d] = st * jnp.exp(tot) + jnp.dot(it["vt"], kh, preferred_element_type=F32)
        acc_ref[d, it["rows"], :] = o

    def body(c, carry):
        items = []
        for u in range(HGRN_GROUP):
            cf = c * HGRN_GROUP + u
            for d in range(2):
                cc = cf if d == 0 else n_chunks - 1 - cf
                rows = pl.ds(pl.multiple_of(cc * C, C), C)
                v = i_ref[0, rows, :]
                items.append(dict(d=d, rows=rows, q=q_ref[0, rows, :], v=v, vt=v.T.astype(BF16)))
        for it in items:
            gates(it)
        for l in range(levels - 1, SUBLANE_LEVELS - 1, -1):
            for it in items:
                block_level(it, l)
        for l in range(SUBLANE_LEVELS):
            for it in items:
                sublane_level(it, l)
        for it in items:
            finish(it)
        return carry

    lax.fori_loop(0, n_chunks // HGRN_GROUP, body, 0)
    og = og_ref[0]
    o = _rms(acc_ref[0] + acc_ref[1], ng_ref[...]) * (og * jax.nn.sigmoid(og))
    o_ref[0] = o.astype(o_ref.dtype)


def hgrn_mixer(zhg, lower_bounds, norm_g, layer):
    B, S, _ = zhg.shape
    tri, level = _hgrn_tables(HGRN_C)
    tri = jnp.asarray(tri, BF16)
    level = jnp.asarray(level)
    depth = lower_bounds.shape[0]
    col = lambda g: pl.BlockSpec((1, S, LANES), lambda b, h: (b, 0, g * HGRN_HEADS + h))
    return pl.pallas_call(
        functools.partial(_hgrn_kernel, layer=layer),
        grid=(B, HGRN_HEADS),
        in_specs=[col(0), col(1), col(2), col(3), col(4),
                  pl.BlockSpec((depth, LANES), lambda b, h: (0, h)),
                  _resident((1, LANES)), _resident(tri.shape), _resident(level.shape)],
        out_specs=pl.BlockSpec((1, S, LANES), lambda b, h: (b, 0, h)),
        out_shape=jax.ShapeDtypeStruct((B, S, HGRN_HEADS * LANES), BF16),
        scratch_shapes=[pltpu.VMEM((2, S, LANES), F32), pltpu.VMEM((2, LANES, LANES), F32)],
        compiler_params=_cparams(("parallel", "parallel")),
        name="hgrn",
    )(zhg, zhg, zhg, zhg, zhg, lower_bounds, norm_g.reshape(1, LANES), tri, level)


def _merge_kernel(h_ref, u_ref, ya_ref, yb_ref, yc_ref, wg_ref, wa_ref, wb_ref, wc_ref, wo_ref,
                  o_ref):
    D = h_ref.shape[1]
    u = u_ref[...]
    merged = None
    for j, (y_ref, w_ref) in enumerate(((ya_ref, wa_ref), (yb_ref, wb_ref), (yc_ref, wc_ref))):
        gate = jax.nn.sigmoid(jnp.dot(u, wg_ref[:, j * D:(j + 1) * D], preferred_element_type=F32))
        term = gate * jnp.dot(y_ref[...], w_ref[...], preferred_element_type=F32)
        merged = term if merged is None else merged + term
    o_ref[...] = h_ref[...] + jnp.dot(merged.astype(BF16), wo_ref[...], preferred_element_type=F32)


def merge_block(h, u, ya, yb, yc, wg, wa, wb, wc, wo):
    T, D = h.shape
    tm = min(FFN_TM, T)
    row = lambda w: pl.BlockSpec((tm, w), lambda i: (i, 0))
    return pl.pallas_call(
        _merge_kernel,
        grid=(T // tm,),
        in_specs=[row(D), row(D), row(ya.shape[1]), row(yb.shape[1]), row(yc.shape[1]),
                  _resident(wg.shape), _resident(wa.shape), _resident(wb.shape),
                  _resident(wc.shape), _resident(wo.shape)],
        out_specs=row(D),
        out_shape=jax.ShapeDtypeStruct((T, D), F32),
        compiler_params=_cparams(("parallel",)),
        name="merge",
    )(h, u, ya, yb, yc, wg, wa, wb, wc, wo)


def kernel(x, ffn1_norm, ffn1_w_gate, ffn1_w_up, ffn1_w_down, mix_norm, w_in, diff_lambda_q1, diff_lambda_k1, diff_lambda_q2, diff_lambda_k2, diff_subln, hgrn_lower_bounds, hgrn_norm, w_branch_a, w_branch_b, w_branch_c, w_out, ffn2_norm, ffn2_w_gate, ffn2_w_up, ffn2_w_down, final_norm):
    B, S, D = x.shape
    T = B * S
    depth = w_in.shape[0]
    bf = lambda w: w.astype(BF16)
    strips_a, strips_b = _bias_strips(S, min(ATT_TQ, S))
    col = np.arange(ATT_W)
    is_q = (col < 512) | ((col >= 3 * 512) & (col < 4 * 512))
    col_scale = jnp.asarray(np.where(is_q, LOG2E * HEAD_DIM ** -0.5, 1.0), F32)
    h = x.reshape(T, D)
    for l in range(depth):
        h, u = ffn_block(h, ffn1_norm[l], bf(ffn1_w_gate[l]), bf(ffn1_w_up[l]), bf(ffn1_w_down[l]),
                         mix_norm[l], "hn", BF16)
        w_att = bf(w_in[l][:, :ATT_W] * col_scale)
        w_hg = bf(w_in[l][:, ATT_W:ATT_W + HG_W])
        w_gate = bf(w_in[l][:, ATT_W + HG_W:])
        zatt = project(u, w_att, BF16, tn=ATT_W // 2).reshape(B, S, ATT_W)
        zhg = project(u, w_hg, F32, tn=HG_W // 2).reshape(B, S, HG_W)
        lam_init = 0.8 - 0.6 * float(np.exp(-0.3 * l))
        lamv = jnp.stack([diff_lambda_q1[l], diff_lambda_k1[l], diff_lambda_q2[l], diff_lambda_k2[l]])
        subln = diff_subln[l].reshape(1, 2 * HEAD_DIM)
        ya = attention_pairs(zatt, strips_a, lamv, subln, mode="diff", lam_init=lam_init,
                             q_blk=0, k_blk=4, v_blk=8)
        yb = attention_pairs(zatt, strips_b, lamv, subln, mode="dil", lam_init=lam_init,
                             q_blk=12, k_blk=16, v_blk=20)
        yc = hgrn_mixer(zhg, hgrn_lower_bounds, hgrn_norm[l], l)
        h = merge_block(h, u, ya.reshape(T, -1), yb.reshape(T, -1), yc.reshape(T, -1),
                        w_gate, bf(w_branch_a[l]), bf(w_branch_b[l]),
                        bf(w_branch_c[l]), bf(w_out[l]))
        (h,) = ffn_block(h, ffn2_norm[l], bf(ffn2_w_gate[l]), bf(ffn2_w_up[l]), bf(ffn2_w_down[l]),
                         final_norm, "n" if l == depth - 1 else "h")
    return h.reshape(B, S, D)
```

```python
import functools
import itertools

import numpy as np
import jax
import jax.numpy as jnp
from jax import lax
from jax.experimental import pallas as pl
from jax.experimental.pallas import tpu as pltpu

F32 = jnp.float32
BF16 = jnp.bfloat16

D_MODEL = 1024
HEAD_DIM = 64
DIFF_HEADS = 4
DIL_HEADS = 8
DIL_PATTERNS = ((128, 1), (512, 4), (2048, 16))
HGRN_HEADS = 4
HGRN_DK = 128
N_BRANCH = 3
D_FF = 2816
NORM_EPS = 1e-6
N_ALIBI = DIFF_HEADS + DIL_HEADS
ALIBI_A_IDX = (0, 3, 6, 9)
ALIBI_B_IDX = (1, 2, 4, 5, 7, 8, 10, 11)
ATT_W = 6 * 512
HG_W = 5 * 512

LANES = 128
SUBLANE_LEVELS = 3
BF16_ROWS = 16
VMEM_LIMIT = 56 * 1024 * 1024

FFN_TM = 512
FFN_SUB = 2
MM_TM = 1024
ATT_TQ = 512
ATT_TILES = 4
ATT_KC = 512
HGRN_C = 128
HGRN_GROUP = 8
NEG_BIG = -1e30
LOG2E = float(np.log2(np.e))


def _cparams(sem):
    return pltpu.CompilerParams(dimension_semantics=sem, vmem_limit_bytes=VMEM_LIMIT)


def _resident(shape):
    nd = len(shape)
    return pl.BlockSpec(shape, lambda *_: (0,) * nd, pipeline_mode=pl.Buffered(1))


def _rms(x, g):
    return x * lax.rsqrt(jnp.mean(x * x, axis=-1, keepdims=True) + NORM_EPS) * g


def _ffn_kernel(h_ref, g1_ref, wg_ref, wu_ref, wd_ref, g2_ref, *out_refs, emit):
    ts = h_ref.shape[0] // FFN_SUB
    rows = [slice(s * ts, (s + 1) * ts) for s in range(FFN_SUB)]
    h = [h_ref[r, :] for r in rows]
    n = [_rms(x, g1_ref[...]).astype(BF16) for x in h]
    a = [jnp.dot(x, wg_ref[...], preferred_element_type=F32) for x in n]
    b = [jnp.dot(x, wu_ref[...], preferred_element_type=F32) for x in n]
    t = [(x * jax.nn.sigmoid(x) * y).astype(BF16) for x, y in zip(a, b)]
    hout = [x + 0.5 * jnp.dot(y, wd_ref[...], preferred_element_type=F32) for x, y in zip(h, t)]
    for r, x in zip(rows, hout):
        if "h" in emit:
            out_refs[0][r, :] = x
        if "n" in emit:
            out_refs[-1][r, :] = _rms(x, g2_ref[...]).astype(out_refs[-1].dtype)


def ffn_block(h, g1, wg, wu, wd, g2, emit, n_dtype=F32):
    T, D = h.shape
    tm = min(FFN_TM, T)
    row = pl.BlockSpec((tm, D), lambda i: (i, 0))
    out_dtypes = [F32] * ("h" in emit) + [n_dtype] * ("n" in emit)
    return pl.pallas_call(
        functools.partial(_ffn_kernel, emit=emit),
        grid=(T // tm,),
        in_specs=[row, _resident((1, D)), _resident(wg.shape), _resident(wu.shape),
                  _resident(wd.shape), _resident((1, D))],
        out_specs=[row] * len(out_dtypes),
        out_shape=[jax.ShapeDtypeStruct((T, D), dt) for dt in out_dtypes],
        compiler_params=_cparams(("parallel",)),
        name="ffn",
    )(h, g1.reshape(1, D), wg, wu, wd, g2.reshape(1, D))


def _proj_kernel(x_ref, w_ref, o_ref):
    o_ref[...] = jnp.dot(x_ref[...], w_ref[...], preferred_element_type=F32).astype(o_ref.dtype)


def project(x, w, out_dtype, tn):
    T, K = x.shape
    N = w.shape[1]
    tm = min(MM_TM, T)
    return pl.pallas_call(
        _proj_kernel,
        grid=(T // tm, N // tn),
        in_specs=[pl.BlockSpec((tm, K), lambda i, j: (i, 0)),
                  pl.BlockSpec((K, tn), lambda i, j: (0, j))],
        out_specs=pl.BlockSpec((tm, tn), lambda i, j: (i, j)),
        out_shape=jax.ShapeDtypeStruct((T, N), out_dtype),
        compiler_params=_cparams(("parallel", "arbitrary")),
        name="proj_in",
    )(x, w)


def _attn_kernel(q_ref, k_ref, v_ref, strip_ref, lamv_ref, subln_ref, o_ref, vx_ref, s_ref, *,
                 mode, lam_init, reach):
    tq = strip_ref.shape[2]
    n_tiles = q_ref.shape[1] // tq
    S = k_ref.shape[1]
    qi = pl.program_id(2)

    @pl.when(qi == 0)
    def _():
        vx_ref[:, :LANES] = v_ref[0]
        vx_ref[:, LANES:] = jnp.ones((S, LANES), BF16)

    lane = lax.broadcasted_iota(jnp.int32, (tq, LANES), 1)
    first = lane < HEAD_DIM
    n_strip = strip_ref.shape[1]
    kc = ATT_KC
    n_kc = S // kc
    if mode == "diff":
        lv = lamv_ref[...]
        lam = (jnp.exp(jnp.sum(lv[0:1] * lv[1:2], axis=-1, keepdims=True))
               - jnp.exp(jnp.sum(lv[2:3] * lv[3:4], axis=-1, keepdims=True)) + lam_init)

    def stacked_q(t):
        q = q_ref[0, t * tq:(t + 1) * tq, :]
        zero = jnp.zeros_like(q)
        return jnp.concatenate([jnp.where(first, q, zero), jnp.where(first, zero, q)], axis=0)

    def score_chunk(t, q2, c, m):
        sc = lax.dot_general(q2, k_ref[0, c * kc:(c + 1) * kc, :], (((1,), (1,)), ((), ())),
                             preferred_element_type=F32)
        off = pl.multiple_of(S - (qi * n_tiles + t) * tq + c * kc, LANES)
        bias = [strip_ref[0, e % n_strip, :, pl.ds(off, kc)] for e in range(n_strip)]
        sc = sc + jnp.concatenate([bias[0], bias[-1]], axis=0)
        s_ref[t % 2, :, c * kc:(c + 1) * kc] = sc
        mc = jnp.max(sc, axis=-1, keepdims=True)
        return mc if m is None else jnp.maximum(m, mc)

    def value_chunk(t, c, m, ox):
        p = jnp.exp2(s_ref[t % 2, :, c * kc:(c + 1) * kc] - m).astype(BF16)
        for e in range(2):
            part = jnp.dot(p[e * tq:(e + 1) * tq], vx_ref[c * kc:(c + 1) * kc, :],
                           preferred_element_type=F32)
            ox[e] = part if ox[e] is None else ox[e] + part

    def finish(t, ox):
        outs = [ox[e][:, :LANES] / ox[e][:, LANES:] for e in range(2)]
        if mode == "diff":
            o = outs[0] - lam * outs[1]
            o = _rms(o, subln_ref[...]) * (1.0 - lam_init)
        else:
            o = jnp.where(first, outs[0], outs[1])
        o_ref[0, t * tq:(t + 1) * tq, :] = o.astype(o_ref.dtype)

    def chunks(t):
        if reach is None or n_tiles * tq != S:
            return list(range(n_kc))
        lo, hi = t * tq - reach, (t + 1) * tq - 1 + reach
        return [c for c in range(n_kc) if (c + 1) * kc - 1 >= lo and c * kc <= hi]

    q2 = stacked_q(0)
    m = None
    for c in chunks(0):
        m = score_chunk(0, q2, c, m)
    for t in range(n_tiles):
        ox = [None, None]
        nxt = chunks(t + 1) if t + 1 < n_tiles else []
        if nxt:
            q2_next, m_next = stacked_q(t + 1), None
        for c, cn in itertools.zip_longest(chunks(t), nxt):
            if c is not None:
                value_chunk(t, c, m, ox)
            if cn is not None:
                m_next = score_chunk(t + 1, q2_next, cn, m_next)
        finish(t, ox)
        if nxt:
            m = m_next


def attention_pairs(zatt, strips, lamv, subln, *, mode, lam_init, q_blk, k_blk, v_blk):
    B, S, _ = zatt.shape
    n_pairs = strips.shape[0]
    tq = strips.shape[2]
    n_tiles = min(ATT_TILES, S // tq)
    rows = n_tiles * tq
    reach = max((w // (2 * d)) * d for w, d in DIL_PATTERNS) if mode == "dil" else None
    return pl.pallas_call(
        functools.partial(_attn_kernel, mode=mode, lam_init=lam_init, reach=reach),
        grid=(n_pairs, B, S // rows),
        in_specs=[
            pl.BlockSpec((1, rows, LANES), lambda p, b, i: (b, i, q_blk + p)),
            pl.BlockSpec((1, S, LANES), lambda p, b, i: (b, 0, k_blk + p)),
            pl.BlockSpec((1, S, LANES), lambda p, b, i: (b, 0, v_blk + p)),
            pl.BlockSpec((1, strips.shape[1], tq, 2 * S), lambda p, b, i: (p, 0, 0, 0),
                         pipeline_mode=pl.Buffered(1)),
            _resident(lamv.shape),
            _resident(subln.shape),
        ],
        out_specs=pl.BlockSpec((1, rows, LANES), lambda p, b, i: (b, i, p)),
        out_shape=jax.ShapeDtypeStruct((B, S, n_pairs * LANES), BF16),
        scratch_shapes=[pltpu.VMEM((S, 2 * LANES), BF16), pltpu.VMEM((2, 2 * tq, S), F32)],
        compiler_params=_cparams(("parallel", "parallel", "arbitrary")),
        name="attn_" + mode,
    )(zatt, zatt, zatt, strips, lamv, subln)


def _alibi_slopes():
    h = np.arange(1, N_ALIBI + 1, dtype=np.float64)
    s = np.exp2(-8.0 * h / N_ALIBI)
    return s[list(ALIBI_A_IDX)], s[list(ALIBI_B_IDX)]


def _bias_strips(S, tq):
    sl_a, sl_b = _alibi_slopes()
    r = np.arange(tq, dtype=np.float64)[:, None]
    c = np.arange(2 * S, dtype=np.float64)[None, :]
    delta = np.abs(r - (c - S))
    strips_a = np.stack([(-s * delta)[None] for s in sl_a])
    mult = np.zeros_like(delta)
    for window, dilation in DIL_PATTERNS:
        reach = (window // (2 * dilation)) * dilation
        mult += ((delta % dilation) == 0) & (delta <= reach)
    logm = np.where(mult > 0, np.log(np.maximum(mult, 1.0)), NEG_BIG)
    strips_b = np.stack([np.stack([-sl_b[2 * p] * delta + logm, -sl_b[2 * p + 1] * delta + logm])
                         for p in range(DIL_HEADS // 2)])
    return jnp.asarray(strips_a * LOG2E, F32), jnp.asarray(strips_b * LOG2E, F32)


def _hgrn_tables(C):
    levels = int(np.log2(C))
    t = np.arange(C)
    tri = np.stack([t[None, :] <= t[:, None], t[None, :] >= t[:, None]]).astype(np.float32)
    level = -np.ones((2, C, C), np.int32)
    for l in range(levels):
        m = 1 << l
        upper = (t % (2 * m)) >= m
        same = (t[:, None] // (2 * m)) == (t[None, :] // (2 * m))
        level[0][same & upper[:, None] & (~upper)[None, :]] = l
        level[1][same & (~upper)[:, None] & upper[None, :]] = l
    return tri, level


def _hgrn_kernel(q_ref, ff_ref, fb_ref, i_ref, og_ref, lbp_ref, ng_ref, tri_ref, level_ref,
                 o_ref, acc_ref, st_ref, *, layer):
    S = q_ref.shape[1]
    C = HGRN_C
    n_chunks = S // C
    levels = int(np.log2(C))
    row = lax.broadcasted_iota(jnp.int32, (C, LANES), 0)

    lbp = lbp_ref[...]
    e = jnp.exp(lbp - jnp.max(lbp, axis=0, keepdims=True))
    pr = e / jnp.sum(e, axis=0, keepdims=True)
    lb = jnp.zeros((1, LANES), F32)
    for j in range(1, layer + 1):
        lb = lb + pr[j:j + 1]

    st_ref[...] = jnp.zeros_like(st_ref)

    nt = (((1,), (1,)), ((), ()))

    def gates(it):
        x = (ff_ref if it["d"] == 0 else fb_ref)[0, it["rows"], :]
        f = lb + (1.0 - lb) * jax.nn.sigmoid(x)
        g = jnp.log(f)
        it["kk"] = 1.0 - f
        it["qb"], it["kb"] = it["q"].astype(BF16), it["kk"].astype(BF16)
        g0 = g.astype(BF16)
        r1 = g - g0.astype(F32)
        g1 = r1.astype(BF16)
        g2 = (r1 - g1.astype(F32)).astype(BF16)
        tri = tri_ref[it["d"]]
        it["run"] = (jnp.dot(tri, g0, preferred_element_type=F32)
                     + jnp.dot(tri, g1, preferred_element_type=F32)
                     + jnp.dot(tri, g2, preferred_element_type=F32)) * LOG2E
        it["edge"] = it["run"]
        it["a"] = None

    def block_level(it, l):
        d, run = it["d"], it["run"]
        m = 1 << l
        dt = BF16 if m % BF16_ROWS == 0 else F32
        q, kk = (it["qb"], it["kb"]) if dt == BF16 else (it["q"], it["kk"])
        zeros = jnp.zeros((m, LANES), dt)
        bnd = m - 1 if d == 0 else m
        qparts, kparts = [], []
        for s in range(0, C, 2 * m):
            lo, hi = slice(s, s + m), slice(s + m, s + 2 * m)
            mid = run[s + bnd:s + bnd + 1]
            if d == 0:
                ex_lo = jnp.exp2(mid - run[lo]).astype(dt)
                ex_hi = jnp.exp2(run[hi] - mid).astype(dt)
            else:
                ex_lo = jnp.exp2(run[lo] - mid).astype(dt)
                ex_hi = jnp.exp2(mid - run[hi]).astype(dt)
            if d == 0:
                qparts += [zeros, q[hi] * ex_hi]
                kparts += [kk[lo] * ex_lo, zeros]
            else:
                qparts += [q[lo] * ex_lo, zeros]
                kparts += [zeros, kk[hi] * ex_hi]
        r = lax.dot_general(jnp.concatenate(qparts, axis=0).astype(BF16),
                            jnp.concatenate(kparts, axis=0).astype(BF16), nt,
                            preferred_element_type=F32)
        it["a"] = r if it["a"] is None else jnp.where(level_ref[d] == l, r, it["a"])

    def sublane_level(it, l):
        d, edge = it["d"], it["edge"]
        m = 1 << l
        upper = (row & m) != 0
        dn = pltpu.roll(edge, m, 0)
        up = pltpu.roll(edge, C - m, 0)
        if d == 0:
            mid = jnp.where(upper, dn, edge)
            it["edge"] = jnp.where(upper, edge, up)
        else:
            mid = jnp.where(upper, edge, up)
            it["edge"] = jnp.where(upper, dn, edge)
        ex = jnp.exp2(-jnp.abs(it["run"] - mid)).astype(BF16)
        r = lax.dot_general(it["qb"] * ex, it["kb"] * ex, nt, preferred_element_type=F32)
        it["a"] = jnp.where(level_ref[d] == l, r, it["a"])

    def finish(it):
        d, q, kk, run, v = it["d"], it["q"], it["kk"], it["run"], it["v"]
        tot = run[C - 1:C] if d == 0 else run[0:1]
        st = st_ref[d]
        qe = it["qb"] * jnp.exp2(run).astype(BF16)
        o = lax.dot_general(qe, st.astype(BF16), nt, preferred_element_type=F32)
        o = o + jnp.dot(it["a"].astype(BF16), v.astype(BF16), preferred_element_type=F32)
        o = o + jnp.sum(q * kk, axis=-1, keepdims=True) * v
        kh = it["kb"] * jnp.exp2(tot - run).astype(BF16)
        st_ref[d] = st * jnp.exp2(tot) + jnp.dot(it["vt"], kh, preferred_element_type=F32)
        acc_ref[d, it["rows"], :] = o

    def body(c, carry):
        items = []
        for u in range(HGRN_GROUP):
            cf = c * HGRN_GROUP + u
            for d in range(2):
                cc = cf if d == 0 else n_chunks - 1 - cf
                rows = pl.ds(pl.multiple_of(cc * C, C), C)
                v = i_ref[0, rows, :]
                items.append(dict(d=d, rows=rows, q=q_ref[0, rows, :], v=v, vt=v.T.astype(BF16)))
        for it in items:
            gates(it)
        for l in range(levels - 1, SUBLANE_LEVELS - 1, -1):
            for it in items:
                block_level(it, l)
        for l in range(SUBLANE_LEVELS):
            for it in items:
                sublane_level(it, l)
        for it in items:
            finish(it)
        return carry

    lax.fori_loop(0, n_chunks // HGRN_GROUP, body, 0)
    og = og_ref[0]
    o = _rms(acc_ref[0] + acc_ref[1], ng_ref[...]) * (og * jax.nn.sigmoid(og))
    o_ref[0] = o.astype(o_ref.dtype)


def hgrn_mixer(zhg, lower_bounds, norm_g, layer):
    B, S, _ = zhg.shape
    tri, level = _hgrn_tables(HGRN_C)
    tri = jnp.asarray(tri, BF16)
    level = jnp.asarray(level)
    depth = lower_bounds.shape[0]
    col = lambda g: pl.BlockSpec((1, S, LANES), lambda b, h: (b, 0, g * HGRN_HEADS + h))
    return pl.pallas_call(
        functools.partial(_hgrn_kernel, layer=layer),
        grid=(B, HGRN_HEADS),
        in_specs=[col(0), col(1), col(2), col(3), col(4),
                  pl.BlockSpec((depth, LANES), lambda b, h: (0, h)),
                  _resident((1, LANES)), _resident(tri.shape), _resident(level.shape)],
        out_specs=pl.BlockSpec((1, S, LANES), lambda b, h: (b, 0, h)),
        out_shape=jax.ShapeDtypeStruct((B, S, HGRN_HEADS * LANES), BF16),
        scratch_shapes=[pltpu.VMEM((2, S, LANES), F32), pltpu.VMEM((2, LANES, LANES), F32)],
        compiler_params=_cparams(("parallel", "parallel")),
        name="hgrn",
    )(zhg, zhg, zhg, zhg, zhg, lower_bounds, norm_g.reshape(1, LANES), tri, level)


def _merge_kernel(h_ref, u_ref, ya_ref, yb_ref, yc_ref, wg_ref, wa_ref, wb_ref, wc_ref, wo_ref,
                  o_ref):
    D = h_ref.shape[1]
    u = u_ref[...]
    merged = None
    for j, (y_ref, w_ref) in enumerate(((ya_ref, wa_ref), (yb_ref, wb_ref), (yc_ref, wc_ref))):
        gate = jax.nn.sigmoid(jnp.dot(u, wg_ref[:, j * D:(j + 1) * D], preferred_element_type=F32))
        term = gate * jnp.dot(y_ref[...], w_ref[...], preferred_element_type=F32)
        merged = term if merged is None else merged + term
    o_ref[...] = h_ref[...] + jnp.dot(merged.astype(BF16), wo_ref[...], preferred_element_type=F32)


def merge_block(h, u, ya, yb, yc, wg, wa, wb, wc, wo):
    T, D = h.shape
    tm = min(FFN_TM, T)
    row = lambda w: pl.BlockSpec((tm, w), lambda i: (i, 0))
    return pl.pallas_call(
        _merge_kernel,
        grid=(T // tm,),
        in_specs=[row(D), row(D), row(ya.shape[1]), row(yb.shape[1]), row(yc.shape[1]),
                  _resident(wg.shape), _resident(wa.shape), _resident(wb.shape),
                  _resident(wc.shape), _resident(wo.shape)],
        out_specs=row(D),
        out_shape=jax.ShapeDtypeStruct((T, D), F32),
        compiler_params=_cparams(("parallel",)),
        name="merge",
    )(h, u, ya, yb, yc, wg, wa, wb, wc, wo)


def kernel(x, ffn1_norm, ffn1_w_gate, ffn1_w_up, ffn1_w_down, mix_norm, w_in, diff_lambda_q1, diff_lambda_k1, diff_lambda_q2, diff_lambda_k2, diff_subln, hgrn_lower_bounds, hgrn_norm, w_branch_a, w_branch_b, w_branch_c, w_out, ffn2_norm, ffn2_w_gate, ffn2_w_up, ffn2_w_down, final_norm):
    B, S, D = x.shape
    T = B * S
    depth = w_in.shape[0]
    bf = lambda w: w.astype(BF16)
    strips_a, strips_b = _bias_strips(S, min(ATT_TQ, S))
    col = np.arange(ATT_W)
    is_q = (col < 512) | ((col >= 3 * 512) & (col < 4 * 512))
    col_scale = jnp.asarray(np.where(is_q, LOG2E * HEAD_DIM ** -0.5, 1.0), F32)
    h = x.reshape(T, D)
    for l in range(depth):
        h, u = ffn_block(h, ffn1_norm[l], bf(ffn1_w_gate[l]), bf(ffn1_w_up[l]), bf(ffn1_w_down[l]),
                         mix_norm[l], "hn", BF16)
        w_att = bf(w_in[l][:, :ATT_W] * col_scale)
        w_hg = bf(w_in[l][:, ATT_W:ATT_W + HG_W])
        w_gate = bf(w_in[l][:, ATT_W + HG_W:])
        zatt = project(u, w_att, BF16, tn=ATT_W // 2).reshape(B, S, ATT_W)
        zhg = project(u, w_hg, F32, tn=HG_W // 2).reshape(B, S, HG_W)
        lam_init = 0.8 - 0.6 * float(np.exp(-0.3 * l))
        lamv = jnp.stack([diff_lambda_q1[l], diff_lambda_k1[l], diff_lambda_q2[l], diff_lambda_k2[l]])
        subln = diff_subln[l].reshape(1, 2 * HEAD_DIM)
        ya = attention_pairs(zatt, strips_a, lamv, subln, mode="diff", lam_init=lam_init,
                             q_blk=0, k_blk=4, v_blk=8)
        yb = attention_pairs(zatt, strips_b, lamv, subln, mode="dil", lam_init=lam_init,
                             q_blk=12, k_blk=16, v_blk=20)
        yc = hgrn_mixer(zhg, hgrn_lower_bounds, hgrn_norm[l], l)
        h = merge_block(h, u, ya.reshape(T, -1), yb.reshape(T, -1), yc.reshape(T, -1),
                        w_gate, bf(w_branch_a[l]), bf(w_branch_b[l]),
                        bf(w_branch_c[l]), bf(w_out[l]))
        (h,) = ffn_block(h, ffn2_norm[l], bf(ffn2_w_gate[l]), bf(ffn2_w_up[l]), bf(ffn2_w_down[l]),
                         final_norm, "n" if l == depth - 1 else "h")
    return h.reshape(B, S, D)
```

```python
import functools
import itertools

import numpy as np
import jax
import jax.numpy as jnp
from jax import lax
from jax.experimental import pallas as pl
from jax.experimental.pallas import tpu as pltpu

F32 = jnp.float32
BF16 = jnp.bfloat16

D_MODEL = 1024
HEAD_DIM = 64
DIFF_HEADS = 4
DIL_HEADS = 8
DIL_PATTERNS = ((128, 1), (512, 4), (2048, 16))
HGRN_HEADS = 4
HGRN_DK = 128
N_BRANCH = 3
D_FF = 2816
NORM_EPS = 1e-6
N_ALIBI = DIFF_HEADS + DIL_HEADS
ALIBI_A_IDX = (0, 3, 6, 9)
ALIBI_B_IDX = (1, 2, 4, 5, 7, 8, 10, 11)
ATT_W = 6 * 512
HG_W = 5 * 512

LANES = 128
SUBLANE_LEVELS = 3
BF16_ROWS = 16
VMEM_LIMIT = 56 * 1024 * 1024

FFN_TM = 512
FFN_SUB = 2
MM_TM = 1024
ATT_TQ = 512
ATT_TILES = 4
ATT_KC = 512
HGRN_C = 128
HGRN_GROUP = 16
NEG_BIG = -1e30
LOG2E = float(np.log2(np.e))


def _cparams(sem):
    return pltpu.CompilerParams(dimension_semantics=sem, vmem_limit_bytes=VMEM_LIMIT)


def _resident(shape):
    nd = len(shape)
    return pl.BlockSpec(shape, lambda *_: (0,) * nd, pipeline_mode=pl.Buffered(1))


def _rms(x, g):
    return x * lax.rsqrt(jnp.mean(x * x, axis=-1, keepdims=True) + NORM_EPS) * g


def _ffn_kernel(h_ref, g1_ref, wg_ref, wu_ref, wd_ref, g2_ref, *out_refs, emit):
    ts = h_ref.shape[0] // FFN_SUB
    rows = [slice(s * ts, (s + 1) * ts) for s in range(FFN_SUB)]
    h = [h_ref[r, :] for r in rows]
    n = [_rms(x, g1_ref[...]).astype(BF16) for x in h]
    a = [jnp.dot(x, wg_ref[...], preferred_element_type=F32) for x in n]
    b = [jnp.dot(x, wu_ref[...], preferred_element_type=F32) for x in n]
    t = [(x * jax.nn.sigmoid(x) * y).astype(BF16) for x, y in zip(a, b)]
    hout = [x + 0.5 * jnp.dot(y, wd_ref[...], preferred_element_type=F32) for x, y in zip(h, t)]
    for r, x in zip(rows, hout):
        if "h" in emit:
            out_refs[0][r, :] = x
        if "n" in emit:
            out_refs[-1][r, :] = _rms(x, g2_ref[...]).astype(out_refs[-1].dtype)


def ffn_block(h, g1, wg, wu, wd, g2, emit, n_dtype=F32):
    T, D = h.shape
    tm = min(FFN_TM, T)
    row = pl.BlockSpec((tm, D), lambda i: (i, 0))
    out_dtypes = [F32] * ("h" in emit) + [n_dtype] * ("n" in emit)
    return pl.pallas_call(
        functools.partial(_ffn_kernel, emit=emit),
        grid=(T // tm,),
        in_specs=[row, _resident((1, D)), _resident(wg.shape), _resident(wu.shape),
                  _resident(wd.shape), _resident((1, D))],
        out_specs=[row] * len(out_dtypes),
        out_shape=[jax.ShapeDtypeStruct((T, D), dt) for dt in out_dtypes],
        compiler_params=_cparams(("parallel",)),
        name="ffn",
    )(h, g1.reshape(1, D), wg, wu, wd, g2.reshape(1, D))


def _proj_kernel(x_ref, w_ref, o_ref):
    o_ref[...] = jnp.dot(x_ref[...], w_ref[...], preferred_element_type=F32).astype(o_ref.dtype)


def project(x, w, out_dtype, tn):
    T, K = x.shape
    N = w.shape[1]
    tm = min(MM_TM, T)
    return pl.pallas_call(
        _proj_kernel,
        grid=(T // tm, N // tn),
        in_specs=[pl.BlockSpec((tm, K), lambda i, j: (i, 0)),
                  pl.BlockSpec((K, tn), lambda i, j: (0, j))],
        out_specs=pl.BlockSpec((tm, tn), lambda i, j: (i, j)),
        out_shape=jax.ShapeDtypeStruct((T, N), out_dtype),
        compiler_params=_cparams(("parallel", "arbitrary")),
        name="proj_in",
    )(x, w)


def _attn_kernel(q_ref, k_ref, v_ref, strip_ref, lamv_ref, subln_ref, o_ref, vx_ref, s_ref, *,
                 mode, lam_init, reach):
    tq = strip_ref.shape[2]
    n_tiles = q_ref.shape[1] // tq
    S = k_ref.shape[1]
    qi = pl.program_id(2)

    @pl.when(qi == 0)
    def _():
        vx_ref[:, :LANES] = v_ref[0]
        vx_ref[:, LANES:] = jnp.ones((S, LANES), BF16)

    lane = lax.broadcasted_iota(jnp.int32, (tq, LANES), 1)
    first = lane < HEAD_DIM
    n_strip = strip_ref.shape[1]
    kc = ATT_KC
    n_kc = S // kc
    if mode == "diff":
        lv = lamv_ref[...]
        lam = (jnp.exp(jnp.sum(lv[0:1] * lv[1:2], axis=-1, keepdims=True))
               - jnp.exp(jnp.sum(lv[2:3] * lv[3:4], axis=-1, keepdims=True)) + lam_init)

    def stacked_q(t):
        q = q_ref[0, t * tq:(t + 1) * tq, :]
        zero = jnp.zeros_like(q)
        return jnp.concatenate([jnp.where(first, q, zero), jnp.where(first, zero, q)], axis=0)

    def score_chunk(t, q2, c, m):
        sc = lax.dot_general(q2, k_ref[0, c * kc:(c + 1) * kc, :], (((1,), (1,)), ((), ())),
                             preferred_element_type=F32)
        off = pl.multiple_of(S - (qi * n_tiles + t) * tq + c * kc, LANES)
        bias = [strip_ref[0, e % n_strip, :, pl.ds(off, kc)] for e in range(n_strip)]
        sc = sc + jnp.concatenate([bias[0], bias[-1]], axis=0)
        s_ref[t % 2, :, c * kc:(c + 1) * kc] = sc
        mc = jnp.max(sc, axis=-1, keepdims=True)
        return mc if m is None else jnp.maximum(m, mc)

    def value_chunk(t, c, m, ox):
        p = jnp.exp2(s_ref[t % 2, :, c * kc:(c + 1) * kc] - m).astype(BF16)
        for e in range(2):
            part = jnp.dot(p[e * tq:(e + 1) * tq], vx_ref[c * kc:(c + 1) * kc, :],
                           preferred_element_type=F32)
            ox[e] = part if ox[e] is None else ox[e] + part

    def finish(t, ox):
        outs = [ox[e][:, :LANES] / ox[e][:, LANES:] for e in range(2)]
        if mode == "diff":
            o = outs[0] - lam * outs[1]
            o = _rms(o, subln_ref[...]) * (1.0 - lam_init)
        else:
            o = jnp.where(first, outs[0], outs[1])
        o_ref[0, t * tq:(t + 1) * tq, :] = o.astype(o_ref.dtype)

    def chunks(t):
        if reach is None or n_tiles * tq != S:
            return list(range(n_kc))
        lo, hi = t * tq - reach, (t + 1) * tq - 1 + reach
        return [c for c in range(n_kc) if (c + 1) * kc - 1 >= lo and c * kc <= hi]

    q2 = stacked_q(0)
    m = None
    for c in chunks(0):
        m = score_chunk(0, q2, c, m)
    for t in range(n_tiles):
        ox = [None, None]
        nxt = chunks(t + 1) if t + 1 < n_tiles else []
        if nxt:
            q2_next, m_next = stacked_q(t + 1), None
        for c, cn in itertools.zip_longest(chunks(t), nxt):
            if c is not None:
                value_chunk(t, c, m, ox)
            if cn is not None:
                m_next = score_chunk(t + 1, q2_next, cn, m_next)
        finish(t, ox)
        if nxt:
            m = m_next


def attention_pairs(zatt, strips, lamv, subln, *, mode, lam_init, q_blk, k_blk, v_blk):
    B, S, _ = zatt.shape
    n_pairs = strips.shape[0]
    tq = strips.shape[2]
    n_tiles = min(ATT_TILES, S // tq)
    rows = n_tiles * tq
    reach = max((w // (2 * d)) * d for w, d in DIL_PATTERNS) if mode == "dil" else None
    return pl.pallas_call(
        functools.partial(_attn_kernel, mode=mode, lam_init=lam_init, reach=reach),
        grid=(n_pairs, B, S // rows),
        in_specs=[
            pl.BlockSpec((1, rows, LANES), lambda p, b, i: (b, i, q_blk + p)),
            pl.BlockSpec((1, S, LANES), lambda p, b, i: (b, 0, k_blk + p)),
            pl.BlockSpec((1, S, LANES), lambda p, b, i: (b, 0, v_blk + p)),
            pl.BlockSpec((1, strips.shape[1], tq, 2 * S), lambda p, b, i: (p, 0, 0, 0),
                         pipeline_mode=pl.Buffered(1)),
            _resident(lamv.shape),
            _resident(subln.shape),
        ],
        out_specs=pl.BlockSpec((1, rows, LANES), lambda p, b, i: (b, i, p)),
        out_shape=jax.ShapeDtypeStruct((B, S, n_pairs * LANES), BF16),
        scratch_shapes=[pltpu.VMEM((S, 2 * LANES), BF16), pltpu.VMEM((2, 2 * tq, S), F32)],
        compiler_params=_cparams(("parallel", "parallel", "arbitrary")),
        name="attn_" + mode,
    )(zatt, zatt, zatt, strips, lamv, subln)


def _alibi_slopes():
    h = np.arange(1, N_ALIBI + 1, dtype=np.float64)
    s = np.exp2(-8.0 * h / N_ALIBI)
    return s[list(ALIBI_A_IDX)], s[list(ALIBI_B_IDX)]


def _bias_strips(S, tq):
    sl_a, sl_b = _alibi_slopes()
    r = np.arange(tq, dtype=np.float64)[:, None]
    c = np.arange(2 * S, dtype=np.float64)[None, :]
    delta = np.abs(r - (c - S))
    strips_a = np.stack([(-s * delta)[None] for s in sl_a])
    mult = np.zeros_like(delta)
    for window, dilation in DIL_PATTERNS:
        reach = (window // (2 * dilation)) * dilation
        mult += ((delta % dilation) == 0) & (delta <= reach)
    logm = np.where(mult > 0, np.log(np.maximum(mult, 1.0)), NEG_BIG)
    strips_b = np.stack([np.stack([-sl_b[2 * p] * delta + logm, -sl_b[2 * p + 1] * delta + logm])
                         for p in range(DIL_HEADS // 2)])
    return jnp.asarray(strips_a * LOG2E, F32), jnp.asarray(strips_b * LOG2E, F32)


def _hgrn_tables(C):
    levels = int(np.log2(C))
    t = np.arange(C)
    tri = np.stack([t[None, :] <= t[:, None], t[None, :] >= t[:, None]]).astype(np.float32)
    level = -np.ones((2, C, C), np.int32)
    for l in range(levels):
        m = 1 << l
        upper = (t % (2 * m)) >= m
        same = (t[:, None] // (2 * m)) == (t[None, :] // (2 * m))
        level[0][same & upper[:, None] & (~upper)[None, :]] = l
        level[1][same & (~upper)[:, None] & upper[None, :]] = l
    return tri, level


def _hgrn_kernel(q_ref, ff_ref, fb_ref, i_ref, og_ref, lbp_ref, ng_ref, tri_ref, level_ref,
                 o_ref, acc_ref, st_ref, *, layer):
    S = q_ref.shape[1]
    C = HGRN_C
    n_chunks = S // C
    levels = int(np.log2(C))
    row = lax.broadcasted_iota(jnp.int32, (C, LANES), 0)

    lbp = lbp_ref[...]
    e = jnp.exp(lbp - jnp.max(lbp, axis=0, keepdims=True))
    pr = e / jnp.sum(e, axis=0, keepdims=True)
    lb = jnp.zeros((1, LANES), F32)
    for j in range(1, layer + 1):
        lb = lb + pr[j:j + 1]

    st_ref[...] = jnp.zeros_like(st_ref)

    nt = (((1,), (1,)), ((), ()))

    def gates(it):
        x = (ff_ref if it["d"] == 0 else fb_ref)[0, it["rows"], :]
        f = lb + (1.0 - lb) * jax.nn.sigmoid(x)
        g = jnp.log(f)
        it["kk"] = 1.0 - f
        it["qb"], it["kb"] = it["q"].astype(BF16), it["kk"].astype(BF16)
        g0 = g.astype(BF16)
        r1 = g - g0.astype(F32)
        g1 = r1.astype(BF16)
        g2 = (r1 - g1.astype(F32)).astype(BF16)
        tri = tri_ref[it["d"]]
        it["run"] = (jnp.dot(tri, g0, preferred_element_type=F32)
                     + jnp.dot(tri, g1, preferred_element_type=F32)
                     + jnp.dot(tri, g2, preferred_element_type=F32)) * LOG2E
        it["edge"] = it["run"]
        it["a"] = None

    def block_level(it, l):
        d, run = it["d"], it["run"]
        m = 1 << l
        dt = BF16 if m % BF16_ROWS == 0 else F32
        q, kk = (it["qb"], it["kb"]) if dt == BF16 else (it["q"], it["kk"])
        zeros = jnp.zeros((m, LANES), dt)
        bnd = m - 1 if d == 0 else m
        qparts, kparts = [], []
        for s in range(0, C, 2 * m):
            lo, hi = slice(s, s + m), slice(s + m, s + 2 * m)
            mid = run[s + bnd:s + bnd + 1]
            if d == 0:
                ex_lo = jnp.exp2(mid - run[lo]).astype(dt)
                ex_hi = jnp.exp2(run[hi] - mid).astype(dt)
            else:
                ex_lo = jnp.exp2(run[lo] - mid).astype(dt)
                ex_hi = jnp.exp2(mid - run[hi]).astype(dt)
            if d == 0:
                qparts += [zeros, q[hi] * ex_hi]
                kparts += [kk[lo] * ex_lo, zeros]
            else:
                qparts += [q[lo] * ex_lo, zeros]
                kparts += [zeros, kk[hi] * ex_hi]
        r = lax.dot_general(jnp.concatenate(qparts, axis=0).astype(BF16),
                            jnp.concatenate(kparts, axis=0).astype(BF16), nt,
                            preferred_element_type=F32)
        it["a"] = r if it["a"] is None else jnp.where(level_ref[d] == l, r, it["a"])

    def sublane_level(it, l):
        d, edge = it["d"], it["edge"]
        m = 1 << l
        upper = (row & m) != 0
        dn = pltpu.roll(edge, m, 0)
        up = pltpu.roll(edge, C - m, 0)
        if d == 0:
            mid = jnp.where(upper, dn, edge)
            it["edge"] = jnp.where(upper, edge, up)
        else:
            mid = jnp.where(upper, edge, up)
            it["edge"] = jnp.where(upper, dn, edge)
        ex = jnp.exp2(-jnp.abs(it["run"] - mid)).astype(BF16)
        r = lax.dot_general(it["qb"] * ex, it["kb"] * ex, nt, preferred_element_type=F32)
        it["a"] = jnp.where(level_ref[d] == l, r, it["a"])

    def finish(it):
        d, q, kk, run, v = it["d"], it["q"], it["kk"], it["run"], it["v"]
        tot = run[C - 1:C] if d == 0 else run[0:1]
        st = st_ref[d]
        qe = it["qb"] * jnp.exp2(run).astype(BF16)
        o = lax.dot_general(qe, st.astype(BF16), nt, preferred_element_type=F32)
        o = o + jnp.dot(it["a"].astype(BF16), v.astype(BF16), preferred_element_type=F32)
        o = o + jnp.sum(q * kk, axis=-1, keepdims=True) * v
        kh = it["kb"] * jnp.exp2(tot - run).astype(BF16)
        st_ref[d] = st * jnp.exp2(tot) + jnp.dot(it["vt"], kh, preferred_element_type=F32)
        acc_ref[d, it["rows"], :] = o

    def body(c, carry):
        items = []
        for u in range(HGRN_GROUP):
            cf = c * HGRN_GROUP + u
            for d in range(2):
                cc = cf if d == 0 else n_chunks - 1 - cf
                rows = pl.ds(pl.multiple_of(cc * C, C), C)
                v = i_ref[0, rows, :]
                items.append(dict(d=d, rows=rows, q=q_ref[0, rows, :], v=v, vt=v.T.astype(BF16)))
        for it in items:
            gates(it)
        for l in range(levels - 1, SUBLANE_LEVELS - 1, -1):
            for it in items:
                block_level(it, l)
        for l in range(SUBLANE_LEVELS):
            for it in items:
                sublane_level(it, l)
        for it in items:
            finish(it)
        return carry

    lax.fori_loop(0, n_chunks // HGRN_GROUP, body, 0)
    og = og_ref[0]
    o = _rms(acc_ref[0] + acc_ref[1], ng_ref[...]) * (og * jax.nn.sigmoid(og))
    o_ref[0] = o.astype(o_ref.dtype)


def hgrn_mixer(zhg, lower_bounds, norm_g, layer):
    B, S, _ = zhg.shape
    tri, level = _hgrn_tables(HGRN_C)
    tri = jnp.asarray(tri, BF16)
    level = jnp.asarray(level)
    depth = lower_bounds.shape[0]
    col = lambda g: pl.BlockSpec((1, S, LANES), lambda b, h: (b, 0, g * HGRN_HEADS + h))
    return pl.pallas_call(
        functools.partial(_hgrn_kernel, layer=layer),
        grid=(B, HGRN_HEADS),
        in_specs=[col(0), col(1), col(2), col(3), col(4),
                  pl.BlockSpec((depth, LANES), lambda b, h: (0, h)),
                  _resident((1, LANES)), _resident(tri.shape), _resident(level.shape)],
        out_specs=pl.BlockSpec((1, S, LANES), lambda b, h: (b, 0, h)),
        out_shape=jax.ShapeDtypeStruct((B, S, HGRN_HEADS * LANES), BF16),
        scratch_shapes=[pltpu.VMEM((2, S, LANES), F32), pltpu.VMEM((2, LANES, LANES), F32)],
        compiler_params=_cparams(("parallel", "parallel")),
        name="hgrn",
    )(zhg, zhg, zhg, zhg, zhg, lower_bounds, norm_g.reshape(1, LANES), tri, level)


def _merge_kernel(h_ref, u_ref, ya_ref, yb_ref, yc_ref, wg_ref, wa_ref, wb_ref, wc_ref, wo_ref,
                  o_ref):
    D = h_ref.shape[1]
    u = u_ref[...]
    merged = None
    for j, (y_ref, w_ref) in enumerate(((ya_ref, wa_ref), (yb_ref, wb_ref), (yc_ref, wc_ref))):
        gate = jax.nn.sigmoid(jnp.dot(u, wg_ref[:, j * D:(j + 1) * D], preferred_element_type=F32))
        term = gate * jnp.dot(y_ref[...], w_ref[...], preferred_element_type=F32)
        merged = term if merged is None else merged + term
    o_ref[...] = h_ref[...] + jnp.dot(merged.astype(BF16), wo_ref[...], preferred_element_type=F32)


def merge_block(h, u, ya, yb, yc, wg, wa, wb, wc, wo):
    T, D = h.shape
    tm = min(FFN_TM, T)
    row = lambda w: pl.BlockSpec((tm, w), lambda i: (i, 0))
    return pl.pallas_call(
        _merge_kernel,
        grid=(T // tm,),
        in_specs=[row(D), row(D), row(ya.shape[1]), row(yb.shape[1]), row(yc.shape[1]),
                  _resident(wg.shape), _resident(wa.shape), _resident(wb.shape),
                  _resident(wc.shape), _resident(wo.shape)],
        out_specs=row(D),
        out_shape=jax.ShapeDtypeStruct((T, D), F32),
        compiler_params=_cparams(("parallel",)),
        name="merge",
    )(h, u, ya, yb, yc, wg, wa, wb, wc, wo)


def kernel(x, ffn1_norm, ffn1_w_gate, ffn1_w_up, ffn1_w_down, mix_norm, w_in, diff_lambda_q1, diff_lambda_k1, diff_lambda_q2, diff_lambda_k2, diff_subln, hgrn_lower_bounds, hgrn_norm, w_branch_a, w_branch_b, w_branch_c, w_out, ffn2_norm, ffn2_w_gate, ffn2_w_up, ffn2_w_down, final_norm):
    B, S, D = x.shape
    T = B * S
    depth = w_in.shape[0]
    bf = lambda w: w.astype(BF16)
    strips_a, strips_b = _bias_strips(S, min(ATT_TQ, S))
    col = np.arange(ATT_W)
    is_q = (col < 512) | ((col >= 3 * 512) & (col < 4 * 512))
    col_scale = jnp.asarray(np.where(is_q, LOG2E * HEAD_DIM ** -0.5, 1.0), F32)
    h = x.reshape(T, D)
    for l in range(depth):
        h, u = ffn_block(h, ffn1_norm[l], bf(ffn1_w_gate[l]), bf(ffn1_w_up[l]), bf(ffn1_w_down[l]),
                         mix_norm[l], "hn", BF16)
        w_att = bf(w_in[l][:, :ATT_W] * col_scale)
        w_hg = bf(w_in[l][:, ATT_W:ATT_W + HG_W])
        w_gate = bf(w_in[l][:, ATT_W + HG_W:])
        zatt = project(u, w_att, BF16, tn=ATT_W // 2).reshape(B, S, ATT_W)
        zhg = project(u, w_hg, F32, tn=HG_W // 2).reshape(B, S, HG_W)
        lam_init = 0.8 - 0.6 * float(np.exp(-0.3 * l))
        lamv = jnp.stack([diff_lambda_q1[l], diff_lambda_k1[l], diff_lambda_q2[l], diff_lambda_k2[l]])
        subln = diff_subln[l].reshape(1, 2 * HEAD_DIM)
        ya = attention_pairs(zatt, strips_a, lamv, subln, mode="diff", lam_init=lam_init,
                             q_blk=0, k_blk=4, v_blk=8)
        yb = attention_pairs(zatt, strips_b, lamv, subln, mode="dil", lam_init=lam_init,
                             q_blk=12, k_blk=16, v_blk=20)
        yc = hgrn_mixer(zhg, hgrn_lower_bounds, hgrn_norm[l], l)
        h = merge_block(h, u, ya.reshape(T, -1), yb.reshape(T, -1), yc.reshape(T, -1),
                        w_gate, bf(w_branch_a[l]), bf(w_branch_b[l]),
                        bf(w_branch_c[l]), bf(w_out[l]))
        (h,) = ffn_block(h, ffn2_norm[l], bf(ffn2_w_gate[l]), bf(ffn2_w_up[l]), bf(ffn2_w_down[l]),
                         final_norm, "n" if l == depth - 1 else "h")
    return h.reshape(B, S, D)
```
